```python
import math
import jax, jax.numpy as jnp
from jax import lax
import numpy as np

D_MODEL = 1024
BATCH = 8
SEQ = 4096
DEPTH = 2

EPS = 1e-6
D_FF = 2816
GLA_HEADS = 4
GLA_DK = 128
GLA_DV = 256
GLA_GATE_RANK = 16
GLA_GATE_TAU = 16.0
GLA_CHUNK = 64
GLA_QK = GLA_HEADS * GLA_DK
GLA_V = GLA_HEADS * GLA_DV
SSD_HEADS = 16
SSD_HEADDIM = 64
SSD_STATE = 128
SSD_GROUPS = 2
SSD_CONV = 4
SSD_CHUNK = 128
SSD_INNER = SSD_HEADS * SSD_HEADDIM
SSD_BC = SSD_GROUPS * SSD_STATE
SSD_CONV_CH = SSD_INNER + 2 * SSD_BC
HY_IN_SIZES = (GLA_QK, GLA_QK, GLA_V, GLA_V, GLA_GATE_RANK, SSD_INNER, SSD_CONV_CH, SSD_HEADS)
HY_IN_WIDTH = 2 * GLA_QK + 2 * GLA_V + GLA_GATE_RANK + SSD_INNER + SSD_CONV_CH + SSD_HEADS
MIX_WIDTH = GLA_V + SSD_INNER
DSA_HEADS = 16
DSA_Q_RANK = 256
DSA_LATENT = 128
DSA_VDIM = 64
IDX_HEADS = 16
IDX_DIM = 64
TOPK_MAX = 256
Q_BLOCK = 128
REL_BUCKETS = 32
REL_MAX_DIST = 128

kernel_name = 'hybrid_gla_ssd_dsa_macaron'


def rmsnorm(x, g):
    xf = x.astype(jnp.float32)
    y = xf * lax.rsqrt(jnp.mean(xf * xf, axis=-1, keepdims=True) + EPS)
    return (y * g.astype(jnp.float32)).astype(x.dtype)


def layernorm(x, g, b):
    xf = x.astype(jnp.float32)
    mu = jnp.mean(xf, axis=-1, keepdims=True)
    var = jnp.mean(jnp.square(xf - mu), axis=-1, keepdims=True)
    y = (xf - mu) * lax.rsqrt(var + EPS)
    return (y * g.astype(jnp.float32) + b.astype(jnp.float32)).astype(x.dtype)


def swiglu(h, wg, wu, wd):
    return (jax.nn.silu(h @ wg) * (h @ wu)) @ wd


def split_cols(t, sizes):
    idx = np.cumsum(np.array(sizes))[:-1].tolist()
    return jnp.split(t, idx, axis=-1)


def to_chunks(t, c):
    b, s = t.shape[0], t.shape[1]
    return jnp.moveaxis(t.reshape(b, s // c, c, *t.shape[2:]), 1, 0)


def from_chunks(t):
    t = jnp.moveaxis(t, 0, 1)
    return t.reshape(t.shape[0], t.shape[1] * t.shape[2], *t.shape[3:])


def causal_depthwise_conv(x, w, b):
    k, ch = w.shape
    y = lax.conv_general_dilated(x, w.reshape(k, 1, ch).astype(x.dtype), window_strides=(1,),
                                 padding=[(k - 1, 0)], dimension_numbers=('NWC', 'WIO', 'NWC'),
                                 feature_group_count=ch)
    return y + b.astype(x.dtype)


def gla_chunked(q, k, v, log_a):
    bsz, _, h, dk = q.shape
    dv = v.shape[-1]
    c = GLA_CHUNK
    mask = jnp.tril(jnp.ones((c, c), dtype=bool))[None, :, :, None, None]

    def step(state, inp):
        qc, kc, vc, gc = [t.astype(jnp.float32) for t in inp]
        b = jnp.cumsum(gc, axis=1)
        o_inter = jnp.einsum('bihk,bhkv->bihv', qc * jnp.exp(b), state)
        diff = b[:, :, None] - b[:, None]
        decay = jnp.exp(jnp.where(mask, diff, -jnp.inf))
        attn = jnp.einsum('bijhk,bjhk->bijh', qc[:, :, None] * decay, kc)
        o_intra = jnp.einsum('bijh,bjhv->bihv', attn, vc)
        b_last = b[:, -1]
        k_dec = kc * jnp.exp(b_last[:, None] - b)
        state = state * jnp.exp(b_last)[..., None] + jnp.einsum('bjhk,bjhv->bhkv', k_dec, vc)
        return state, o_inter + o_intra

    state0 = jnp.zeros((bsz, h, dk, dv), jnp.float32)
    _, o = lax.scan(step, state0, (to_chunks(q, c), to_chunks(k, c), to_chunks(v, c), to_chunks(log_a, c)))
    return from_chunks(o).astype(v.dtype)


def ssd_chunked(x, dt, a, bm, cm):
    bsz, _, h, p = x.shape
    n = bm.shape[-1]
    rep = h // bm.shape[2]
    c = SSD_CHUNK
    mask = jnp.tril(jnp.ones((c, c), dtype=bool))[None, :, :, None]

    def step(state, inp):
        xc, dtc, bc, cc = [t.astype(jnp.float32) for t in inp]
        bc = jnp.repeat(bc, rep, axis=2)
        cc = jnp.repeat(cc, rep, axis=2)
        acum = jnp.cumsum(dtc * a.astype(jnp.float32), axis=1)
        diff = acum[:, :, None] - acum[:, None]
        seg = jnp.exp(jnp.where(mask, diff, -jnp.inf))
        xdt = xc * dtc[..., None]
        scores = jnp.einsum('bihn,bjhn->bijh', cc, bc) * seg
        y_intra = jnp.einsum('bijh,bjhp->bihp', scores, xdt)
        y_inter = jnp.einsum('bihn,bhpn->bihp', cc * jnp.exp(acum)[..., None], state)
        a_last = acum[:, -1]
        w = jnp.exp(a_last[:, None] - acum)[..., None]
        state = state * jnp.exp(a_last)[..., None, None] + jnp.einsum('bjhn,bjhp->bhpn', bc * w, xdt)
        return state, y_intra + y_inter

    state0 = jnp.zeros((bsz, h, p, n), jnp.float32)
    _, y = lax.scan(step, state0, (to_chunks(x, c), to_chunks(dt, c), to_chunks(bm, c), to_chunks(cm, c)))
    return from_chunks(y).astype(x.dtype)


def gla_ssd_mixer(h, w_in, gla_w_gate2, gla_b_gate, gla_norm_g, ssd_conv_w, ssd_conv_b,
                  ssd_dt_bias, ssd_a_log, ssd_d, ssd_norm_g, w_out):
    bsz, s, _ = h.shape
    q, k, v, r, g_lr, z, xbc, dt = split_cols(h @ w_in, HY_IN_SIZES)
    q = q.reshape(bsz, s, GLA_HEADS, GLA_DK) * (GLA_DK ** -0.5)
    k = k.reshape(bsz, s, GLA_HEADS, GLA_DK)
    v = v.reshape(bsz, s, GLA_HEADS, GLA_DV)
    log_a = jax.nn.log_sigmoid((g_lr @ gla_w_gate2 + gla_b_gate).astype(jnp.float32)) / GLA_GATE_TAU
    o_gla = gla_chunked(q, k, v, log_a.reshape(bsz, s, GLA_HEADS, GLA_DK))
    o_gla = rmsnorm(o_gla, gla_norm_g.reshape(GLA_HEADS, GLA_DV)).reshape(bsz, s, GLA_V)
    o_gla = o_gla * jax.nn.silu(r)
    xbc = jax.nn.silu(causal_depthwise_conv(xbc, ssd_conv_w, ssd_conv_b))
    xs, bm, cm = split_cols(xbc, (SSD_INNER, SSD_BC, SSD_BC))
    xs = xs.reshape(bsz, s, SSD_HEADS, SSD_HEADDIM)
    dt = jax.nn.softplus((dt + ssd_dt_bias).astype(jnp.float32))
    a = -jnp.exp(ssd_a_log.astype(jnp.float32))
    y = ssd_chunked(xs, dt, a, bm.reshape(bsz, s, SSD_GROUPS, SSD_STATE), cm.reshape(bsz, s, SSD_GROUPS, SSD_STATE))
    y = (y + ssd_d[:, None].astype(y.dtype) * xs).reshape(bsz, s, SSD_INNER) * jax.nn.silu(z)
    y = rmsnorm(y.reshape(bsz, s, SSD_GROUPS, SSD_INNER // SSD_GROUPS),
                ssd_norm_g.reshape(SSD_GROUPS, SSD_INNER // SSD_GROUPS)).reshape(bsz, s, SSD_INNER)
    return jnp.concatenate([o_gla, y], axis=-1) @ w_out


def t5_bucket(rel):
    max_exact = REL_BUCKETS // 2
    relf = jnp.maximum(rel, 1).astype(jnp.float32)
    large = max_exact + (jnp.log(relf / max_exact) / math.log(REL_MAX_DIST / max_exact)
                         * (REL_BUCKETS - max_exact)).astype(jnp.int32)
    large = jnp.minimum(large, REL_BUCKETS - 1)
    return jnp.where(rel < max_exact, rel, large)


def dsa_mixer(h, w_dq, q_norm_g, w_uq, w_dkv, kv_norm_g, w_uv, w_o,
              idx_w_q, idx_w_k, idx_ln_g, idx_ln_b, idx_w_w, rel_bias):
    bsz, s, _ = h.shape
    q_lat = rmsnorm(h @ w_dq, q_norm_g)
    q = (q_lat @ w_uq).reshape(bsz, s, DSA_HEADS, DSA_LATENT)
    kv = rmsnorm(h @ w_dkv, kv_norm_g)
    q_idx = (q_lat @ idx_w_q).reshape(bsz, s, IDX_HEADS, IDX_DIM)
    k_idx = layernorm(h @ idx_w_k, idx_ln_g, idx_ln_b)
    w_idx = (h @ idx_w_w) * (IDX_HEADS ** -0.5 * IDX_DIM ** -0.5)
    topk = min(TOPK_MAX, s // 4)
    nb = s // Q_BLOCK
    key_pos = jnp.arange(s)

    def one_block(inp):
        qb, qib, wb, start = inp
        qpos = start + jnp.arange(Q_BLOCK)
        idx_logits = jnp.einsum('bqhd,bsd->bqhs', qib, k_idx).astype(jnp.float32)
        score = jnp.einsum('bqhs,bqh->bqs', jax.nn.relu(idx_logits), wb.astype(jnp.float32))
        score = jnp.where((key_pos[None, :] <= qpos[:, None])[None], score, -jnp.inf)
        _, sel = lax.top_k(score, topk)
        valid = sel <= qpos[None, :, None]
        kv_sel = jax.vmap(lambda kvb, ib: kvb[ib])(kv, sel)
        logits = jnp.einsum('bqhc,bqkc->bqhk', qb, kv_sel).astype(jnp.float32) * (DSA_LATENT ** -0.5)
        bias = rel_bias[t5_bucket(jnp.maximum(qpos[None, :, None] - sel, 0))]
        logits = logits + jnp.moveaxis(bias, 3, 2).astype(jnp.float32)
        logits = jnp.where(valid[:, :, None, :], logits, -jnp.inf)
        p = jax.nn.softmax(logits, axis=-1).astype(kv.dtype)
        return jnp.einsum('bqhk,bqkc->bqhc', p, kv_sel)

    o_lat = lax.map(one_block, (to_chunks(q, Q_BLOCK), to_chunks(q_idx, Q_BLOCK),
                                to_chunks(w_idx, Q_BLOCK), jnp.arange(nb) * Q_BLOCK))
    o_lat = from_chunks(o_lat)
    o = jnp.einsum('bshc,hcd->bshd', o_lat, w_uv).reshape(bsz, s, DSA_HEADS * DSA_VDIM)
    return o @ w_o


def setup_inputs(seed: int = 0) -> dict:
    key = jax.random.key(seed)
    keys = jax.random.split(key, 64)
    counter = [0]

    def nk():
        counter[0] += 1
        return keys[counter[0] - 1]

    def w(shape, fan_in):
        return jax.random.normal(nk(), shape, jnp.float32) * (fan_in ** -0.5)

    def gain(shape):
        return 1.0 + 0.02 * jax.random.normal(nk(), shape, jnp.float32)

    def small(shape, scale=0.02):
        return scale * jax.random.normal(nk(), shape, jnp.float32)

    ne, no = (DEPTH + 1) // 2, DEPTH // 2
    x = jax.random.normal(nk(), (BATCH, SEQ, D_MODEL), jnp.float32)
    dt0 = jnp.exp(jax.random.uniform(nk(), (ne, SSD_HEADS), jnp.float32, math.log(1e-3), math.log(1e-1)))
    ssd_dt_bias = dt0 + jnp.log(-jnp.expm1(-dt0))
    ssd_a_log = jnp.log(jax.random.uniform(nk(), (ne, SSD_HEADS), jnp.float32, 1.0, 16.0))
    return {
        'x': x,
        'ffn_norm_g': gain((DEPTH, 2, D_MODEL)),
        'ffn_w_gate': w((DEPTH, 2, D_MODEL, D_FF), D_MODEL),
        'ffn_w_up': w((DEPTH, 2, D_MODEL, D_FF), D_MODEL),
        'ffn_w_down': w((DEPTH, 2, D_FF, D_MODEL), D_FF),
        'mix_norm_g': gain((DEPTH, D_MODEL)),
        'hy_w_in': w((ne, D_MODEL, HY_IN_WIDTH), D_MODEL),
        'gla_w_gate2': w((ne, GLA_GATE_RANK, GLA_QK), GLA_GATE_RANK),
        'gla_b_gate': small((ne, GLA_QK), 0.1),
        'gla_norm_g': gain((ne, GLA_V)),
        'ssd_conv_w': w((ne, SSD_CONV, SSD_CONV_CH), SSD_CONV),
        'ssd_conv_b': small((ne, SSD_CONV_CH)),
        'ssd_dt_bias': ssd_dt_bias,
        'ssd_a_log': ssd_a_log,
        'ssd_d': gain((ne, SSD_HEADS)),
        'ssd_norm_g': gain((ne, SSD_INNER)),
        'hy_w_out': w((ne, MIX_WIDTH, D_MODEL), MIX_WIDTH),
        'dsa_w_dq': w((no, D_MODEL, DSA_Q_RANK), D_MODEL),
        'dsa_q_norm_g': gain((no, DSA_Q_RANK)),
        'dsa_w_uq': w((no, DSA_Q_RANK, DSA_HEADS * DSA_LATENT), DSA_Q_RANK),
        'dsa_w_dkv': w((no, D_MODEL, DSA_LATENT), D_MODEL),
        'dsa_kv_norm_g': gain((no, DSA_LATENT)),
        'dsa_w_uv': w((no, DSA_HEADS, DSA_LATENT, DSA_VDIM), DSA_LATENT),
        'dsa_w_o': w((no, DSA_HEADS * DSA_VDIM, D_MODEL), DSA_HEADS * DSA_VDIM),
        'idx_w_q': w((no, DSA_Q_RANK, IDX_HEADS * IDX_DIM), DSA_Q_RANK),
        'idx_w_k': w((no, D_MODEL, IDX_DIM), D_MODEL),
        'idx_ln_g': gain((no, IDX_DIM)),
        'idx_ln_b': small((no, IDX_DIM)),
        'idx_w_w': w((no, D_MODEL, IDX_HEADS), D_MODEL),
        'rel_bias': small((REL_BUCKETS, DSA_HEADS), 0.5),
        'final_norm_g': gain((D_MODEL,)),
    }


def reference(x, ffn_norm_g, ffn_w_gate, ffn_w_up, ffn_w_down, mix_norm_g,
              hy_w_in, gla_w_gate2, gla_b_gate, gla_norm_g, ssd_conv_w, ssd_conv_b,
              ssd_dt_bias, ssd_a_log, ssd_d, ssd_norm_g, hy_w_out,
              dsa_w_dq, dsa_q_norm_g, dsa_w_uq, dsa_w_dkv, dsa_kv_norm_g, dsa_w_uv, dsa_w_o,
              idx_w_q, idx_w_k, idx_ln_g, idx_ln_b, idx_w_w, rel_bias, final_norm_g):
    for layer in range(DEPTH):
        h = rmsnorm(x, ffn_norm_g[layer, 0])
        x = x + 0.5 * swiglu(h, ffn_w_gate[layer, 0], ffn_w_up[layer, 0], ffn_w_down[layer, 0])
        h = rmsnorm(x, mix_norm_g[layer])
        i = layer // 2
        if layer % 2 == 0:
            x = x + gla_ssd_mixer(h, hy_w_in[i], gla_w_gate2[i], gla_b_gate[i], gla_norm_g[i],
                                  ssd_conv_w[i], ssd_conv_b[i], ssd_dt_bias[i], ssd_a_log[i],
                                  ssd_d[i], ssd_norm_g[i], hy_w_out[i])
        else:
            x = x + dsa_mixer(h, dsa_w_dq[i], dsa_q_norm_g[i], dsa_w_uq[i], dsa_w_dkv[i],
                              dsa_kv_norm_g[i], dsa_w_uv[i], dsa_w_o[i], idx_w_q[i], idx_w_k[i],
                              idx_ln_g[i], idx_ln_b[i], idx_w_w[i], rel_bias)
        h = rmsnorm(x, ffn_norm_g[layer, 1])
        x = x + 0.5 * swiglu(h, ffn_w_gate[layer, 1], ffn_w_up[layer, 1], ffn_w_down[layer, 1])
    return rmsnorm(x, final_norm_g)
```

```python
import functools
import math

import numpy as np
import jax
import jax.numpy as jnp
from jax import lax
from jax.experimental import pallas as pl
from jax.experimental.pallas import tpu as pltpu

F32 = jnp.float32
BF16 = jnp.bfloat16
I32 = jnp.int32

EPS = 1e-6
D_MODEL = 1024
D_FF = 2816
GLA_HEADS = 4
GLA_DK = 128
GLA_DV = 256
GLA_GATE_RANK = 16
GLA_GATE_TAU = 16.0
GLA_CHUNK = 64
GLA_QK = GLA_HEADS * GLA_DK
GLA_V = GLA_HEADS * GLA_DV
SSD_HEADS = 16
SSD_HEADDIM = 64
SSD_STATE = 128
SSD_GROUPS = 2
SSD_CONV = 4
SSD_CHUNK = 128
SSD_INNER = SSD_HEADS * SSD_HEADDIM
SSD_BC = SSD_GROUPS * SSD_STATE
SSD_HEADS_PER_GROUP = SSD_HEADS // SSD_GROUPS
DSA_HEADS = 16
DSA_Q_RANK = 256
DSA_LATENT = 128
DSA_VDIM = 64
IDX_HEADS = 16
IDX_DIM = 64
TOPK_MAX = 256
REL_BUCKETS = 32
REL_MAX_DIST = 128

LANES = 128
SUBLANES = 8
VMEM_LIMIT = 56 * 1024 * 1024

HY_SMALL = LANES
HY_MAIN = 2 * GLA_QK + 2 * GLA_V + SSD_INNER + SSD_INNER + 2 * SSD_BC
HY_WIDTH = HY_MAIN + HY_SMALL
HY_DT_LANE = GLA_GATE_RANK

NEG_BIG = -1e30
INT_MIN = -2 ** 31


def _nt(a, b):
    return lax.dot_general(a, b, (((1,), (1,)), ((), ())), preferred_element_type=F32)


def _tn(a, b):
    return lax.dot_general(a, b, (((0,), (0,)), ((), ())), preferred_element_type=F32)


def _nn(a, b):
    return jnp.dot(a, b, preferred_element_type=F32)


def _silu(a):
    return a / (1.0 + jnp.exp(-a))


def _softplus(a):
    return jnp.maximum(a, 0.0) + jnp.log1p(jnp.exp(-jnp.abs(a)))


def _rms(x, g):
    return x * lax.rsqrt(jnp.mean(x * x, axis=-1, keepdims=True) + EPS) * g


def _params(*sem):
    return pltpu.CompilerParams(dimension_semantics=sem, vmem_limit_bytes=VMEM_LIMIT)


def _ffn_body(x_ref, g_ref, wg_ref, wu_ref, wd_ref, fg_ref, o_ref, h_scr, acc_scr, *, final_norm):
    j = pl.program_id(1)

    @pl.when(j == 0)
    def _():
        h_scr[...] = _rms(x_ref[...], g_ref[...]).astype(BF16)
        acc_scr[...] = jnp.zeros_like(acc_scr)

    h = h_scr[...]
    a = _nn(h, wg_ref[...])
    b = _nn(h, wu_ref[...])
    t = (_silu(a) * b).astype(BF16)
    acc_scr[...] += _nn(t, wd_ref[...])

    @pl.when(j == pl.num_programs(1) - 1)
    def _():
        y = x_ref[...] + 0.5 * acc_scr[...]
        if final_norm:
            y = _rms(y, fg_ref[...])
        o_ref[...] = y


def _ffn(x, g, wg, wu, wd, fg, *, final_norm, tm=512, tf=1408):
    t, d = x.shape
    f = wg.shape[1]
    tm = min(tm, t)
    tf = min(tf, f)
    return pl.pallas_call(
        functools.partial(_ffn_body, final_norm=final_norm),
        grid=(t // tm, f // tf),
        in_specs=[
            pl.BlockSpec((tm, d), lambda i, j: (i, 0)),
            pl.BlockSpec((1, d), lambda i, j: (0, 0)),
            pl.BlockSpec((d, tf), lambda i, j: (0, j)),
            pl.BlockSpec((d, tf), lambda i, j: (0, j)),
            pl.BlockSpec((tf, d), lambda i, j: (j, 0)),
            pl.BlockSpec((1, d), lambda i, j: (0, 0)),
        ],
        out_specs=pl.BlockSpec((tm, d), lambda i, j: (i, 0)),
        out_shape=jax.ShapeDtypeStruct((t, d), F32),
        scratch_shapes=[pltpu.VMEM((tm, d), BF16), pltpu.VMEM((tm, d), F32)],
        compiler_params=_params("parallel", "arbitrary"),
        name="ffn",
    )(x, g, wg, wu, wd, fg)


def _in_proj_body(x_ref, g_ref, w_ref, o_ref, h_scr):
    @pl.when(pl.program_id(1) == 0)
    def _():
        h_scr[...] = _rms(x_ref[...], g_ref[...]).astype(BF16)

    o_ref[...] = _nn(h_scr[...], w_ref[...])


def _in_proj(x, g, w, *, tm=512, tn=1920):
    t, d = x.shape
    n = w.shape[1]
    tm = min(tm, t)
    return pl.pallas_call(
        _in_proj_body,
        grid=(t // tm, n // tn),
        in_specs=[
            pl.BlockSpec((tm, d), lambda i, j: (i, 0)),
            pl.BlockSpec((1, d), lambda i, j: (0, 0)),
            pl.BlockSpec((d, tn), lambda i, j: (0, j)),
        ],
        out_specs=pl.BlockSpec((tm, tn), lambda i, j: (i, j)),
        out_shape=jax.ShapeDtypeStruct((t, n), F32),
        scratch_shapes=[pltpu.VMEM((tm, d), BF16)],
        compiler_params=_params("parallel", "arbitrary"),
        name="hy_in_proj",
    )(x, g, w)


def _gla_body(q_ref, k_ref, v_ref, r_ref, sm_ref, wg2_ref, bg_ref, ng_ref, o_ref, st_scr, *, n_chunks):
    @pl.when(pl.program_id(1) == 0)
    def _():
        st_scr[...] = jnp.zeros_like(st_scr)

    c = GLA_CHUNK
    row = lax.broadcasted_iota(I32, (c, c), 0)
    col = lax.broadcasted_iota(I32, (c, c), 1)
    tril = row >= col
    tri_f = tril.astype(F32)
    scale = GLA_DK ** -0.5

    def chunk(ci, carry):
        rows = pl.ds(pl.multiple_of(ci * c, c), c)
        sm = sm_ref[0, rows, :].astype(BF16)
        gl = _nn(sm, wg2_ref[...]) + bg_ref[...]
        g = -_softplus(-gl) * (1.0 / GLA_GATE_TAU)
        b = jnp.dot(tri_f, g, preferred_element_type=F32, precision=lax.Precision.HIGHEST)
        b_last = b[c - 1:c, :]
        e_b = jnp.exp(b)
        e_nb = jnp.exp(-b)
        e_kd = jnp.exp(b_last - b)
        e_last = jnp.exp(b_last)
        q = q_ref[0, rows, :]
        k = k_ref[0, rows, :]
        for h in range(GLA_HEADS):
            ks = slice(h * GLA_DK, (h + 1) * GLA_DK)
            vs = slice(h * GLA_DV, (h + 1) * GLA_DV)
            qt = (q[:, ks] * e_b[:, ks] * scale).astype(BF16)
            kt = (k[:, ks] * e_nb[:, ks]).astype(BF16)
            kd = (k[:, ks] * e_kd[:, ks]).astype(BF16)
            v = v_ref[0, rows, vs].astype(BF16)
            st = st_scr[h]
            attn = jnp.where(tril, _nt(qt, kt), 0.0).astype(BF16)
            o = _nt(qt, st.astype(BF16)) + _nn(attn, v)
            st_scr[h] = st * e_last[:, ks] + _tn(v, kd)
            o = _rms(o, ng_ref[:, vs])
            o_ref[0, rows, vs] = o * _silu(r_ref[0, rows, vs])
        return carry

    lax.fori_loop(0, n_chunks, chunk, 0)


def _gla(p, wg2, bg, ng, *, cs=512):
    bsz, s, _ = p.shape
    cs = min(cs, s)
    return pl.pallas_call(
        functools.partial(_gla_body, n_chunks=cs // GLA_CHUNK),
        grid=(bsz, s // cs),
        in_specs=[
            pl.BlockSpec((1, cs, GLA_QK), lambda b, i: (b, i, 0)),
            pl.BlockSpec((1, cs, GLA_QK), lambda b, i: (b, i, 1)),
            pl.BlockSpec((1, cs, GLA_V), lambda b, i: (b, i, 1)),
            pl.BlockSpec((1, cs, GLA_V), lambda b, i: (b, i, 2)),
            pl.BlockSpec((1, cs, HY_SMALL), lambda b, i: (b, i, HY_MAIN // HY_SMALL)),
            pl.BlockSpec((HY_SMALL, GLA_QK), lambda b, i: (0, 0)),
            pl.BlockSpec((1, GLA_QK), lambda b, i: (0, 0)),
            pl.BlockSpec((1, GLA_V), lambda b, i: (0, 0)),
        ],
        out_specs=pl.BlockSpec((1, cs, GLA_V), lambda b, i: (b, i, 0)),
        out_shape=jax.ShapeDtypeStruct((bsz, s, GLA_V), F32),
        scratch_shapes=[pltpu.VMEM((GLA_HEADS, GLA_DV, GLA_DK), F32)],
        compiler_params=_params("parallel", "arbitrary"),
        name="gla",
    )(p, p, p, p, p, wg2, bg, ng)


def _ssd_body(z_ref, xs_ref, bc_ref, sm_ref, cwx_ref, cbx_ref, cwb_ref, cbb_ref, dtb_ref, a_ref, d_ref,
              ng_ref, o_ref, xpad, bpad, st_scr, y_scr, *, cs, n_chunks):
    first = pl.program_id(1) == 0
    halo = SUBLANES

    @pl.when(first)
    def _():
        st_scr[...] = jnp.zeros_like(st_scr)
        xpad[0:halo, :] = jnp.zeros((halo, SSD_INNER), F32)
        bpad[0:halo, :] = jnp.zeros((halo, 2 * SSD_BC), F32)

    @pl.when(jnp.logical_not(first))
    def _():
        xpad[0:halo, :] = xpad[cs:cs + halo, :]
        bpad[0:halo, :] = bpad[cs:cs + halo, :]

    xpad[halo:halo + cs, :] = xs_ref[0]
    bpad[halo:halo + cs, :] = bc_ref[0]

    c = SSD_CHUNK
    row = lax.broadcasted_iota(I32, (c, c), 0)
    col = lax.broadcasted_iota(I32, (c, c), 1)
    tril = row >= col
    tri_f = tril.astype(F32)

    def conv(pad_ref, w_ref, b_ref, start):
        win = pad_ref[pl.ds(start, c + halo), :]
        acc = b_ref[...]
        for kk in range(SSD_CONV):
            off = halo - (SSD_CONV - 1) + kk
            acc = acc + w_ref[kk:kk + 1, :] * win[off:off + c, :]
        return _silu(acc)

    def chunk(ci, carry):
        start = pl.multiple_of(ci * c, c)
        rows = pl.ds(start, c)
        xc = conv(xpad, cwx_ref, cbx_ref, start)
        bcc = conv(bpad, cwb_ref, cbb_ref, start)
        dt = _softplus(sm_ref[0, rows, :] + dtb_ref[...])
        da = dt * a_ref[...]
        acum = jnp.dot(tri_f, da, preferred_element_type=F32, precision=lax.Precision.HIGHEST)
        acum_t = acum.T
        for grp in range(SSD_GROUPS):
            bm = bcc[:, grp * SSD_STATE:(grp + 1) * SSD_STATE].astype(BF16)
            cm = bcc[:, SSD_BC + grp * SSD_STATE:SSD_BC + (grp + 1) * SSD_STATE].astype(BF16)
            scores = _nt(cm, bm)
            for hh in range(SSD_HEADS_PER_GROUP):
                h = grp * SSD_HEADS_PER_GROUP + hh
                ln = HY_DT_LANE + h
                hs = slice(h * SSD_HEADDIM, (h + 1) * SSD_HEADDIM)
                a_col = acum[:, ln:ln + 1]
                a_row = acum_t[ln:ln + 1, :]
                a_last = acum[c - 1:c, ln:ln + 1]
                seg = jnp.where(tril, jnp.exp(a_col - a_row), 0.0)
                xh = xc[:, hs]
                xdt = xh * dt[:, ln:ln + 1]
                st = st_scr[h]
                y = _nn((scores * seg).astype(BF16), xdt.astype(BF16))
                y = y + jnp.exp(a_col) * _nt(cm, st.astype(BF16))
                st_scr[h] = st * jnp.exp(a_last) + _tn((xdt * jnp.exp(a_last - a_col)).astype(BF16), bm)
                y_scr[:, hs] = y + d_ref[:, hs] * xh
        y = y_scr[...] * _silu(z_ref[0, rows, :])
        gw = SSD_INNER // SSD_GROUPS
        for grp in range(SSD_GROUPS):
            gs = slice(grp * gw, (grp + 1) * gw)
            o_ref[0, rows, gs] = _rms(y[:, gs], ng_ref[:, gs])
        return carry

    lax.fori_loop(0, n_chunks, chunk, 0)


def _ssd(p, cwx, cbx, cwb, cbb, dtb, a_pad, d_full, ng, *, cs=512):
    bsz, s, _ = p.shape
    cs = min(cs, s)
    full = lambda shape: pl.BlockSpec(shape, lambda b, i: (0,) * len(shape))
    return pl.pallas_call(
        functools.partial(_ssd_body, cs=cs, n_chunks=cs // SSD_CHUNK),
        grid=(bsz, s // cs),
        in_specs=[
            pl.BlockSpec((1, cs, SSD_INNER), lambda b, i: (b, i, 3)),
            pl.BlockSpec((1, cs, SSD_INNER), lambda b, i: (b, i, 4)),
            pl.BlockSpec((1, cs, 2 * SSD_BC), lambda b, i: (b, i, 10)),
            pl.BlockSpec((1, cs, HY_SMALL), lambda b, i: (b, i, HY_MAIN // HY_SMALL)),
            full((SSD_CONV, SSD_INNER)), full((1, SSD_INNER)),
            full((SSD_CONV, 2 * SSD_BC)), full((1, 2 * SSD_BC)),
            full((1, HY_SMALL)), full((1, HY_SMALL)),
            full((1, SSD_INNER)), full((1, SSD_INNER)),
        ],
        out_specs=pl.BlockSpec((1, cs, SSD_INNER), lambda b, i: (b, i, 0)),
        out_shape=jax.ShapeDtypeStruct((bsz, s, SSD_INNER), F32),
        scratch_shapes=[
            pltpu.VMEM((cs + SUBLANES, SSD_INNER), F32),
            pltpu.VMEM((cs + SUBLANES, 2 * SSD_BC), F32),
            pltpu.VMEM((SSD_HEADS, SSD_HEADDIM, SSD_STATE), F32),
            pltpu.VMEM((SSD_CHUNK, SSD_INNER), F32),
        ],
        compiler_params=_params("parallel", "arbitrary"),
        name="ssd",
    )(p, p, p, p, cwx, cbx, cwb, cbb, dtb, a_pad, d_full, ng)


def _out_proj_body(x_ref, a_ref, b_ref, wa_ref, wb_ref, o_ref):
    acc = _nn(a_ref[...].astype(BF16), wa_ref[...])
    acc = acc + _nn(b_ref[...].astype(BF16), wb_ref[...])
    o_ref[...] = x_ref[...] + acc


def _out_proj(x, a, b, wa, wb, *, tm=512):
    t, d = x.shape
    tm = min(tm, t)
    tok = lambda w: pl.BlockSpec((tm, w), lambda i: (i, 0))
    return pl.pallas_call(
        _out_proj_body,
        grid=(t // tm,),
        in_specs=[tok(d), tok(a.shape[1]), tok(b.shape[1]),
                  pl.BlockSpec(wa.shape, lambda i: (0, 0)), pl.BlockSpec(wb.shape, lambda i: (0, 0))],
        out_specs=tok(d),
        out_shape=jax.ShapeDtypeStruct((t, d), F32),
        compiler_params=_params("parallel"),
        name="hy_out_proj",
    )(x, a, b, wa, wb)


DSA_H_WIDTH = 512


def _dsa_proj_body(x_ref, g_ref, wh_ref, wq_ref, qg_ref, kvg_ref, lng_ref, lnb_ref,
                   q_ref, qi_ref, kv_ref, kvt_ref, ki_ref, wt_ref):
    h = _rms(x_ref[...], g_ref[...]).astype(BF16)
    c = _nn(h, wh_ref[...])
    q_lat = _rms(c[:, :DSA_Q_RANK], qg_ref[...]).astype(BF16)
    qq = _nn(q_lat, wq_ref[...])
    nq = DSA_HEADS * DSA_LATENT
    q_ref[...] = (qq[:, :nq] * (DSA_LATENT ** -0.5)).astype(BF16)
    qi_ref[...] = qq[:, nq:].astype(BF16)
    kv = _rms(c[:, DSA_Q_RANK:DSA_Q_RANK + DSA_LATENT], kvg_ref[...])
    kv_ref[...] = kv.astype(BF16)
    kvt_ref[...] = kv.T.astype(BF16)
    tail = c[:, DSA_Q_RANK + DSA_LATENT:]
    kr = tail[:, :IDX_DIM]
    mu = jnp.mean(kr, axis=-1, keepdims=True)
    var = jnp.mean(jnp.square(kr - mu), axis=-1, keepdims=True)
    ki = (kr - mu) * lax.rsqrt(var + EPS) * lng_ref[...] + lnb_ref[...]
    ki_ref[...] = ki.astype(BF16)
    tail_t = (tail * (IDX_HEADS ** -0.5 * IDX_DIM ** -0.5)).T
    wt_ref[...] = tail_t[IDX_DIM:IDX_DIM + IDX_HEADS, :]


def _dsa_proj(x, g, wh, wq, qg, kvg, lng, lnb, *, tm=512):
    t, d = x.shape
    tm = min(tm, t)
    nq = DSA_HEADS * DSA_LATENT
    ni = IDX_HEADS * IDX_DIM
    full = lambda a: pl.BlockSpec(a.shape, lambda i: (0,) * a.ndim)
    tok = lambda w: pl.BlockSpec((tm, w), lambda i: (i, 0))
    return pl.pallas_call(
        _dsa_proj_body,
        grid=(t // tm,),
        in_specs=[tok(d), full(g), full(wh), full(wq), full(qg), full(kvg), full(lng), full(lnb)],
        out_specs=[tok(nq), tok(ni), tok(DSA_LATENT),
                   pl.BlockSpec((DSA_LATENT, tm), lambda i: (0, i)),
                   tok(IDX_DIM),
                   pl.BlockSpec((IDX_HEADS, tm), lambda i: (0, i))],
        out_shape=[
            jax.ShapeDtypeStruct((t, nq), BF16),
            jax.ShapeDtypeStruct((t, ni), BF16),
            jax.ShapeDtypeStruct((t, DSA_LATENT), BF16),
            jax.ShapeDtypeStruct((DSA_LATENT, t), BF16),
            jax.ShapeDtypeStruct((t, IDX_DIM), BF16),
            jax.ShapeDtypeStruct((IDX_HEADS, t), F32),
        ],
        compiler_params=_params("parallel"),
        name="dsa_proj",
    )(x, g, wh, wq, qg, kvg, lng, lnb)


DSA_TQ = 256


def _dsa_attn_body(q_ref, qi_ref, wt_ref, ki_ref, kv_ref, kvt_ref, bias_ref, x_ref, wuv_ref, wo_ref,
                   o_ref, key_scr, acc_scr, m_scr, l_scr, u_scr, *, topk):
    tq = DSA_TQ
    tk = DSA_TQ
    qb = pl.program_id(1)
    krow = lax.broadcasted_iota(I32, (tk, tq), 0)
    qcol = lax.broadcasted_iota(I32, (tk, tq), 1)
    causal = krow <= qcol

    def score_block(kb, diag):
        rows = pl.ds(pl.multiple_of(kb * tk, tk), tk)
        kk = ki_ref[rows, :]
        s = jnp.zeros((tk, tq), F32)
        for h in range(IDX_HEADS):
            z = _nt(kk, qi_ref[:, h * IDX_DIM:(h + 1) * IDX_DIM])
            s = s + jnp.maximum(z, 0.0) * wt_ref[h:h + 1, :]
        if diag:
            s = jnp.where(causal, s, -jnp.inf)
        bits = pltpu.bitcast(s, I32)
        key_scr[rows, :] = bits ^ ((bits >> 31) & 0x7FFFFFFF)

    def score_loop(kb, carry):
        score_block(kb, False)
        return carry

    lax.fori_loop(0, qb, score_loop, 0)
    score_block(qb, True)

    def bit_step(it, thr):
        cand = thr ^ (jnp.int32(1) << (31 - it))

        def count(kb, cnt):
            rows = pl.ds(pl.multiple_of(kb * tk, tk), tk)
            ge = jnp.where(key_scr[rows, :] >= cand, 1, 0)
            return cnt + jnp.sum(ge.reshape(tk // SUBLANES, SUBLANES, tq), axis=0)

        cnt = lax.fori_loop(0, qb + 1, count, jnp.zeros((SUBLANES, tq), I32))
        tot = jnp.sum(cnt, axis=0, keepdims=True)
        return jnp.where(tot >= topk, cand, thr)

    thr = lax.fori_loop(0, 32, bit_step, jnp.full((1, tq), INT_MIN, I32))

    m_scr[...] = jnp.full(m_scr.shape, NEG_BIG, F32)
    l_scr[...] = jnp.zeros_like(l_scr)
    acc_scr[...] = jnp.zeros_like(acc_scr)

    def attn_block(kb, kind):
        rows = pl.ds(pl.multiple_of(kb * tk, tk), tk)
        kvb = kv_ref[rows, :]
        kvtb = kvt_ref[:, rows]
        sel = key_scr[rows, :] >= thr
        if kind == 0:
            sel = jnp.logical_and(sel, causal)
        for h in range(DSA_HEADS):
            s = _nt(kvb, q_ref[:, h * DSA_LATENT:(h + 1) * DSA_LATENT])
            if kind is not None:
                s = s + bias_ref[h, kind]
            s = jnp.where(sel, s, NEG_BIG)
            m_old = m_scr[h]
            m_new = jnp.maximum(m_old, jnp.max(s, axis=0, keepdims=True))
            alpha = jnp.exp(m_old - m_new)
            p = jnp.exp(s - m_new)
            l_scr[h] = alpha * l_scr[h] + jnp.sum(p, axis=0, keepdims=True)
            acc_scr[h] = acc_scr[h] * alpha + _nn(kvtb, p.astype(BF16))
            m_scr[h] = m_new

    def far_loop(kb, carry):
        attn_block(kb, None)
        return carry

    lax.fori_loop(0, jnp.maximum(qb - 1, 0), far_loop, 0)

    @pl.when(qb >= 1)
    def _():
        attn_block(qb - 1, 1)

    attn_block(qb, 0)

    for h in range(DSA_HEADS):
        o_t = (acc_scr[h] / l_scr[h]).astype(BF16)
        u_scr[h * DSA_VDIM:(h + 1) * DSA_VDIM, :] = _nn(wuv_ref[h], o_t).astype(BF16)
    y_t = _nn(wo_ref[...], u_scr[...])
    o_ref[...] = x_ref[...] + y_t.T


def _dsa_attn(x, q, qi, wt, ki, kv, kvt, bias_t, wuv_t, wo_t, *, bsz, s):
    t, d = x.shape
    tq = DSA_TQ
    nq = s // tq
    topk = min(TOPK_MAX, s // 4)
    full = lambda a: pl.BlockSpec(a.shape, lambda b, i: (0,) * a.ndim)
    return pl.pallas_call(
        functools.partial(_dsa_attn_body, topk=topk),
        grid=(bsz, nq),
        in_specs=[
            pl.BlockSpec((tq, q.shape[1]), lambda b, i: (b * nq + i, 0)),
            pl.BlockSpec((tq, qi.shape[1]), lambda b, i: (b * nq + i, 0)),
            pl.BlockSpec((IDX_HEADS, tq), lambda b, i: (0, b * nq + i)),
            pl.BlockSpec((s, IDX_DIM), lambda b, i: (b, 0)),
            pl.BlockSpec((s, DSA_LATENT), lambda b, i: (b, 0)),
            pl.BlockSpec((DSA_LATENT, s), lambda b, i: (0, b)),
            full(bias_t),
            pl.BlockSpec((tq, d), lambda b, i: (b * nq + i, 0)),
            full(wuv_t), full(wo_t),
        ],
        out_specs=pl.BlockSpec((tq, d), lambda b, i: (b * nq + i, 0)),
        out_shape=jax.ShapeDtypeStruct((t, d), F32),
        scratch_shapes=[
            pltpu.VMEM((s, tq), I32),
            pltpu.VMEM((DSA_HEADS, DSA_LATENT, tq), F32),
            pltpu.VMEM((DSA_HEADS, 1, tq), F32),
            pltpu.VMEM((DSA_HEADS, 1, tq), F32),
            pltpu.VMEM((DSA_HEADS * DSA_VDIM, tq), BF16),
        ],
        compiler_params=_params("parallel", "arbitrary"),
        name="dsa_attn",
    )(q, qi, wt, ki, kv, kvt, bias_t, x, wuv_t, wo_t)


def _t5_bucket(rel):
    max_exact = REL_BUCKETS // 2
    relf = jnp.maximum(rel, 1).astype(F32)
    large = max_exact + (jnp.log(relf / max_exact) / math.log(REL_MAX_DIST / max_exact)
                         * (REL_BUCKETS - max_exact)).astype(I32)
    large = jnp.minimum(large, REL_BUCKETS - 1)
    return jnp.where(rel < max_exact, rel, large)


def _bias_tiles(rel_bias):
    tq = DSA_TQ
    kk = jnp.arange(tq)[:, None]
    qq = jnp.arange(tq)[None, :]
    rel = jnp.stack([qq - kk, qq - kk + tq])
    table = rel_bias - rel_bias[REL_BUCKETS - 1][None, :]
    tiles = table[_t5_bucket(jnp.maximum(rel, 0))]
    return jnp.transpose(tiles, (3, 0, 1, 2)).astype(F32)


def _hy_in_weight(w_in):
    sizes = (GLA_QK, GLA_QK, GLA_V, GLA_V, GLA_GATE_RANK, SSD_INNER, SSD_INNER + 2 * SSD_BC, SSD_HEADS)
    offs = np.cumsum((0,) + sizes)
    q, k, v, r, g_lr, z, xbc, dt = [w_in[:, offs[i]:offs[i + 1]] for i in range(len(sizes))]
    pad = jnp.zeros((w_in.shape[0], HY_SMALL - GLA_GATE_RANK - SSD_HEADS), w_in.dtype)
    return jnp.concatenate([q, k, v, r, z, xbc, g_lr, dt, pad], axis=1).astype(BF16)


def kernel(x, ffn_norm_g, ffn_w_gate, ffn_w_up, ffn_w_down, mix_norm_g, hy_w_in, gla_w_gate2, gla_b_gate, gla_norm_g, ssd_conv_w, ssd_conv_b, ssd_dt_bias, ssd_a_log, ssd_d, ssd_norm_g, hy_w_out, dsa_w_dq, dsa_q_norm_g, dsa_w_uq, dsa_w_dkv, dsa_kv_norm_g, dsa_w_uv, dsa_w_o, idx_w_q, idx_w_k, idx_ln_g, idx_ln_b, idx_w_w, rel_bias, final_norm_g):
    bsz, s, d = x.shape
    t = bsz * s
    xt = x.reshape(t, d)
    row = lambda a: a.reshape(1, -1).astype(F32)

    def ffn(xt, layer, half, final):
        return _ffn(xt, row(ffn_norm_g[layer, half]), ffn_w_gate[layer, half].astype(BF16),
                    ffn_w_up[layer, half].astype(BF16), ffn_w_down[layer, half].astype(BF16),
                    row(final_norm_g), final_norm=final)

    xt = ffn(xt, 0, 0, False)
    p = _in_proj(xt, row(mix_norm_g[0]), _hy_in_weight(hy_w_in[0])).reshape(bsz, s, HY_WIDTH)
    wg2 = jnp.zeros((HY_SMALL, GLA_QK), F32).at[:GLA_GATE_RANK].set(gla_w_gate2[0]).astype(BF16)
    o_gla = _gla(p, wg2, row(gla_b_gate[0]), row(gla_norm_g[0]))
    cw, cb = ssd_conv_w[0].astype(F32), ssd_conv_b[0].astype(F32)
    lane_pad = lambda v: jnp.zeros((1, HY_SMALL), F32).at[0, HY_DT_LANE:HY_DT_LANE + SSD_HEADS].set(v)
    y_ssd = _ssd(p, cw[:, :SSD_INNER], row(cb[:SSD_INNER]), cw[:, SSD_INNER:], row(cb[SSD_INNER:]),
                 lane_pad(ssd_dt_bias[0].astype(F32)), lane_pad(-jnp.exp(ssd_a_log[0].astype(F32))),
                 row(jnp.repeat(ssd_d[0], SSD_HEADDIM)), row(ssd_norm_g[0]))
    w_out = hy_w_out[0].astype(BF16)
    xt = _out_proj(xt, o_gla.reshape(t, GLA_V), y_ssd.reshape(t, SSD_INNER), w_out[:GLA_V], w_out[GLA_V:])
    xt = ffn(xt, 0, 1, False)

    xt = ffn(xt, 1, 0, False)
    wh = jnp.concatenate(
        [dsa_w_dq[0], dsa_w_dkv[0], idx_w_k[0], idx_w_w[0],
         jnp.zeros((d, DSA_H_WIDTH - DSA_Q_RANK - DSA_LATENT - IDX_DIM - IDX_HEADS), F32)], axis=1).astype(BF16)
    wq = jnp.concatenate([dsa_w_uq[0], idx_w_q[0]], axis=1).astype(BF16)
    q, qi, kv, kvt, ki, wt = _dsa_proj(xt, row(mix_norm_g[1]), wh, wq, row(dsa_q_norm_g[0]),
                                       row(dsa_kv_norm_g[0]), row(idx_ln_g[0]), row(idx_ln_b[0]))
    wuv_t = jnp.transpose(dsa_w_uv[0], (0, 2, 1)).astype(BF16)
    wo_t = dsa_w_o[0].T.astype(BF16)
    xt = _dsa_attn(xt, q, qi, wt, ki, kv, kvt, _bias_tiles(rel_bias.astype(F32)), wuv_t, wo_t, bsz=bsz, s=s)
    xt = ffn(xt, 1, 1, True)
    return xt.reshape(bsz, s, d)
```

```python
import functools
import math

import numpy as np
import jax
import jax.numpy as jnp
from jax import lax
from jax.experimental import pallas as pl
from jax.experimental.pallas import tpu as pltpu

F32 = jnp.float32
BF16 = jnp.bfloat16
I32 = jnp.int32

EPS = 1e-6
D_MODEL = 1024
D_FF = 2816
GLA_HEADS = 4
GLA_DK = 128
GLA_DV = 256
GLA_GATE_RANK = 16
GLA_GATE_TAU = 16.0
GLA_CHUNK = 64
GLA_QK = GLA_HEADS * GLA_DK
GLA_V = GLA_HEADS * GLA_DV
SSD_HEADS = 16
SSD_HEADDIM = 64
SSD_STATE = 128
SSD_GROUPS = 2
SSD_CONV = 4
SSD_CHUNK = 128
SSD_INNER = SSD_HEADS * SSD_HEADDIM
SSD_BC = SSD_GROUPS * SSD_STATE
SSD_HEADS_PER_GROUP = SSD_HEADS // SSD_GROUPS
DSA_HEADS = 16
DSA_Q_RANK = 256
DSA_LATENT = 128
DSA_VDIM = 64
IDX_HEADS = 16
IDX_DIM = 64
TOPK_MAX = 256
REL_BUCKETS = 32
REL_MAX_DIST = 128

LANES = 128
SUBLANES = 8
VMEM_LIMIT = 56 * 1024 * 1024

HY_SMALL = LANES
HY_MAIN = 2 * GLA_QK + 2 * GLA_V + SSD_INNER + SSD_INNER + 2 * SSD_BC
HY_WIDTH = HY_MAIN + HY_SMALL
HY_DT_LANE = GLA_GATE_RANK

NEG_BIG = -1e30
LOG2E = math.log2(math.e)
INT_MIN = -2 ** 31


def _nt(a, b):
    return lax.dot_general(a, b, (((1,), (1,)), ((), ())), preferred_element_type=F32)


def _tn(a, b):
    return lax.dot_general(a, b, (((0,), (0,)), ((), ())), preferred_element_type=F32)


def _nn(a, b):
    return jnp.dot(a, b, preferred_element_type=F32)


def _silu(a):
    return a / (1.0 + jnp.exp(-a))


def _softplus(a):
    return jnp.maximum(a, 0.0) + jnp.log1p(jnp.exp(-jnp.abs(a)))


def _rms(x, g):
    return x * lax.rsqrt(jnp.mean(x * x, axis=-1, keepdims=True) + EPS) * g


def _params(*sem):
    return pltpu.CompilerParams(dimension_semantics=sem, vmem_limit_bytes=VMEM_LIMIT)


def _ffn_body(x_ref, g_ref, wg_ref, wu_ref, wd_ref, fg_ref, o_ref, h_scr, acc_scr, *, final_norm):
    j = pl.program_id(1)

    @pl.when(j == 0)
    def _():
        h_scr[...] = _rms(x_ref[...], g_ref[...]).astype(BF16)
        acc_scr[...] = jnp.zeros_like(acc_scr)

    h = h_scr[...]
    a = _nn(h, wg_ref[...])
    b = _nn(h, wu_ref[...])
    t = (_silu(a) * b).astype(BF16)
    acc_scr[...] += _nn(t, wd_ref[...])

    @pl.when(j == pl.num_programs(1) - 1)
    def _():
        y = x_ref[...] + 0.5 * acc_scr[...]
        if final_norm:
            y = _rms(y, fg_ref[...])
        o_ref[...] = y


def _ffn(x, g, wg, wu, wd, fg, *, final_norm, tm=512, tf=1408):
    t, d = x.shape
    f = wg.shape[1]
    tm = min(tm, t)
    tf = min(tf, f)
    return pl.pallas_call(
        functools.partial(_ffn_body, final_norm=final_norm),
        grid=(t // tm, f // tf),
        in_specs=[
            pl.BlockSpec((tm, d), lambda i, j: (i, 0)),
            pl.BlockSpec((1, d), lambda i, j: (0, 0)),
            pl.BlockSpec((d, tf), lambda i, j: (0, j)),
            pl.BlockSpec((d, tf), lambda i, j: (0, j)),
            pl.BlockSpec((tf, d), lambda i, j: (j, 0)),
            pl.BlockSpec((1, d), lambda i, j: (0, 0)),
        ],
        out_specs=pl.BlockSpec((tm, d), lambda i, j: (i, 0)),
        out_shape=jax.ShapeDtypeStruct((t, d), F32),
        scratch_shapes=[pltpu.VMEM((tm, d), BF16), pltpu.VMEM((tm, d), F32)],
        compiler_params=_params("parallel", "arbitrary"),
        name="ffn",
    )(x, g, wg, wu, wd, fg)


def _in_proj_body(x_ref, g_ref, w_ref, o_ref, h_scr):
    @pl.when(pl.program_id(1) == 0)
    def _():
        h_scr[...] = _rms(x_ref[...], g_ref[...]).astype(BF16)

    o_ref[...] = _nn(h_scr[...], w_ref[...])


def _in_proj(x, g, w, *, tm=512, tn=1920):
    t, d = x.shape
    n = w.shape[1]
    tm = min(tm, t)
    return pl.pallas_call(
        _in_proj_body,
        grid=(t // tm, n // tn),
        in_specs=[
            pl.BlockSpec((tm, d), lambda i, j: (i, 0)),
            pl.BlockSpec((1, d), lambda i, j: (0, 0)),
            pl.BlockSpec((d, tn), lambda i, j: (0, j)),
        ],
        out_specs=pl.BlockSpec((tm, tn), lambda i, j: (i, j)),
        out_shape=jax.ShapeDtypeStruct((t, n), F32),
        scratch_shapes=[pltpu.VMEM((tm, d), BF16)],
        compiler_params=_params("parallel", "arbitrary"),
        name="hy_in_proj",
    )(x, g, w)


def _gla_body(q_ref, k_ref, v_ref, r_ref, sm_ref, wg2_ref, bg_ref, ng_ref, o_ref, st_scr, *, n_chunks):
    @pl.when(pl.program_id(1) == 0)
    def _():
        st_scr[...] = jnp.zeros_like(st_scr)

    c = GLA_CHUNK
    row = lax.broadcasted_iota(I32, (c, c), 0)
    col = lax.broadcasted_iota(I32, (c, c), 1)
    tril = row >= col
    tri_f = tril.astype(F32)
    scale = GLA_DK ** -0.5

    def chunk(ci, carry):
        rows = pl.ds(pl.multiple_of(ci * c, c), c)
        sm = sm_ref[0, rows, :].astype(BF16)
        gl = _nn(sm, wg2_ref[...]) + bg_ref[...]
        g = -_softplus(-gl) * (1.0 / GLA_GATE_TAU)
        b = jnp.dot(tri_f, g, preferred_element_type=F32, precision=lax.Precision.HIGHEST)
        b_last = b[c - 1:c, :]
        e_b = jnp.exp(b)
        e_nb = jnp.exp(-b)
        e_kd = jnp.exp(b_last - b)
        e_last = jnp.exp(b_last)
        q = q_ref[0, rows, :]
        k = k_ref[0, rows, :]
        for h in range(GLA_HEADS):
            ks = slice(h * GLA_DK, (h + 1) * GLA_DK)
            vs = slice(h * GLA_DV, (h + 1) * GLA_DV)
            qt = (q[:, ks] * e_b[:, ks] * scale).astype(BF16)
            kt = (k[:, ks] * e_nb[:, ks]).astype(BF16)
            kd = (k[:, ks] * e_kd[:, ks]).astype(BF16)
            v = v_ref[0, rows, vs].astype(BF16)
            st = st_scr[h]
            attn = jnp.where(tril, _nt(qt, kt), 0.0).astype(BF16)
            o = _nt(qt, st.astype(BF16)) + _nn(attn, v)
            st_scr[h] = st * e_last[:, ks] + _tn(v, kd)
            o = _rms(o, ng_ref[:, vs])
            o_ref[0, rows, vs] = o * _silu(r_ref[0, rows, vs])
        return carry

    lax.fori_loop(0, n_chunks, chunk, 0)


def _gla(p, wg2, bg, ng, *, cs=512):
    bsz, s, _ = p.shape
    cs = min(cs, s)
    return pl.pallas_call(
        functools.partial(_gla_body, n_chunks=cs // GLA_CHUNK),
        grid=(bsz, s // cs),
        in_specs=[
            pl.BlockSpec((1, cs, GLA_QK), lambda b, i: (b, i, 0)),
            pl.BlockSpec((1, cs, GLA_QK), lambda b, i: (b, i, 1)),
            pl.BlockSpec((1, cs, GLA_V), lambda b, i: (b, i, 1)),
            pl.BlockSpec((1, cs, GLA_V), lambda b, i: (b, i, 2)),
            pl.BlockSpec((1, cs, HY_SMALL), lambda b, i: (b, i, HY_MAIN // HY_SMALL)),
            pl.BlockSpec((HY_SMALL, GLA_QK), lambda b, i: (0, 0)),
            pl.BlockSpec((1, GLA_QK), lambda b, i: (0, 0)),
            pl.BlockSpec((1, GLA_V), lambda b, i: (0, 0)),
        ],
        out_specs=pl.BlockSpec((1, cs, GLA_V), lambda b, i: (b, i, 0)),
        out_shape=jax.ShapeDtypeStruct((bsz, s, GLA_V), F32),
        scratch_shapes=[pltpu.VMEM((GLA_HEADS, GLA_DV, GLA_DK), F32)],
        compiler_params=_params("parallel", "arbitrary"),
        name="gla",
    )(p, p, p, p, p, wg2, bg, ng)


def _ssd_body(z_ref, xs_ref, bc_ref, sm_ref, cwx_ref, cbx_ref, cwb_ref, cbb_ref, dtb_ref, a_ref, d_ref,
              ng_ref, o_ref, xpad, bpad, st_scr, y_scr, *, cs, n_chunks):
    first = pl.program_id(1) == 0
    halo = SUBLANES

    @pl.when(first)
    def _():
        st_scr[...] = jnp.zeros_like(st_scr)
        xpad[0:halo, :] = jnp.zeros((halo, SSD_INNER), F32)
        bpad[0:halo, :] = jnp.zeros((halo, 2 * SSD_BC), F32)

    @pl.when(jnp.logical_not(first))
    def _():
        xpad[0:halo, :] = xpad[cs:cs + halo, :]
        bpad[0:halo, :] = bpad[cs:cs + halo, :]

    xpad[halo:halo + cs, :] = xs_ref[0]
    bpad[halo:halo + cs, :] = bc_ref[0]

    c = SSD_CHUNK
    row = lax.broadcasted_iota(I32, (c, c), 0)
    col = lax.broadcasted_iota(I32, (c, c), 1)
    tril = row >= col
    tri_f = tril.astype(F32)

    def conv(pad_ref, w_ref, b_ref, start):
        win = pad_ref[pl.ds(start, c + halo), :]
        acc = b_ref[...]
        for kk in range(SSD_CONV):
            off = halo - (SSD_CONV - 1) + kk
            acc = acc + w_ref[kk:kk + 1, :] * win[off:off + c, :]
        return _silu(acc)

    def chunk(ci, carry):
        start = pl.multiple_of(ci * c, c)
        rows = pl.ds(start, c)
        xc = conv(xpad, cwx_ref, cbx_ref, start)
        bcc = conv(bpad, cwb_ref, cbb_ref, start)
        dt = _softplus(sm_ref[0, rows, :] + dtb_ref[...])
        da = dt * a_ref[...]
        acum = jnp.dot(tri_f, da, preferred_element_type=F32, precision=lax.Precision.HIGHEST)
        acum_t = acum.T
        for grp in range(SSD_GROUPS):
            bm = bcc[:, grp * SSD_STATE:(grp + 1) * SSD_STATE].astype(BF16)
            cm = bcc[:, SSD_BC + grp * SSD_STATE:SSD_BC + (grp + 1) * SSD_STATE].astype(BF16)
            scores = _nt(cm, bm)
            for hh in range(SSD_HEADS_PER_GROUP):
                h = grp * SSD_HEADS_PER_GROUP + hh
                ln = HY_DT_LANE + h
                hs = slice(h * SSD_HEADDIM, (h + 1) * SSD_HEADDIM)
                a_col = acum[:, ln:ln + 1]
                a_row = acum_t[ln:ln + 1, :]
                a_last = acum[c - 1:c, ln:ln + 1]
                seg = jnp.where(tril, jnp.exp(a_col - a_row), 0.0)
                xh = xc[:, hs]
                xdt = xh * dt[:, ln:ln + 1]
                st = st_scr[h]
                y = _nn((scores * seg).astype(BF16), xdt.astype(BF16))
                y = y + jnp.exp(a_col) * _nt(cm, st.astype(BF16))
                st_scr[h] = st * jnp.exp(a_last) + _tn((xdt * jnp.exp(a_last - a_col)).astype(BF16), bm)
                y_scr[:, hs] = y + d_ref[:, hs] * xh
        y = y_scr[...] * _silu(z_ref[0, rows, :])
        gw = SSD_INNER // SSD_GROUPS
        for grp in range(SSD_GROUPS):
            gs = slice(grp * gw, (grp + 1) * gw)
            o_ref[0, rows, gs] = _rms(y[:, gs], ng_ref[:, gs])
        return carry

    lax.fori_loop(0, n_chunks, chunk, 0)


def _ssd(p, cwx, cbx, cwb, cbb, dtb, a_pad, d_full, ng, *, cs=512):
    bsz, s, _ = p.shape
    cs = min(cs, s)
    full = lambda shape: pl.BlockSpec(shape, lambda b, i: (0,) * len(shape))
    return pl.pallas_call(
        functools.partial(_ssd_body, cs=cs, n_chunks=cs // SSD_CHUNK),
        grid=(bsz, s // cs),
        in_specs=[
            pl.BlockSpec((1, cs, SSD_INNER), lambda b, i: (b, i, 3)),
            pl.BlockSpec((1, cs, SSD_INNER), lambda b, i: (b, i, 4)),
            pl.BlockSpec((1, cs, 2 * SSD_BC), lambda b, i: (b, i, 10)),
            pl.BlockSpec((1, cs, HY_SMALL), lambda b, i: (b, i, HY_MAIN // HY_SMALL)),
            full((SSD_CONV, SSD_INNER)), full((1, SSD_INNER)),
            full((SSD_CONV, 2 * SSD_BC)), full((1, 2 * SSD_BC)),
            full((1, HY_SMALL)), full((1, HY_SMALL)),
            full((1, SSD_INNER)), full((1, SSD_INNER)),
        ],
        out_specs=pl.BlockSpec((1, cs, SSD_INNER), lambda b, i: (b, i, 0)),
        out_shape=jax.ShapeDtypeStruct((bsz, s, SSD_INNER), F32),
        scratch_shapes=[
            pltpu.VMEM((cs + SUBLANES, SSD_INNER), F32),
            pltpu.VMEM((cs + SUBLANES, 2 * SSD_BC), F32),
            pltpu.VMEM((SSD_HEADS, SSD_HEADDIM, SSD_STATE), F32),
            pltpu.VMEM((SSD_CHUNK, SSD_INNER), F32),
        ],
        compiler_params=_params("parallel", "arbitrary"),
        name="ssd",
    )(p, p, p, p, cwx, cbx, cwb, cbb, dtb, a_pad, d_full, ng)


def _out_proj_body(x_ref, a_ref, b_ref, wa_ref, wb_ref, o_ref):
    acc = _nn(a_ref[...].astype(BF16), wa_ref[...])
    acc = acc + _nn(b_ref[...].astype(BF16), wb_ref[...])
    o_ref[...] = x_ref[...] + acc


def _out_proj(x, a, b, wa, wb, *, tm=512):
    t, d = x.shape
    tm = min(tm, t)
    tok = lambda w: pl.BlockSpec((tm, w), lambda i: (i, 0))
    return pl.pallas_call(
        _out_proj_body,
        grid=(t // tm,),
        in_specs=[tok(d), tok(a.shape[1]), tok(b.shape[1]),
                  pl.BlockSpec(wa.shape, lambda i: (0, 0)), pl.BlockSpec(wb.shape, lambda i: (0, 0))],
        out_specs=tok(d),
        out_shape=jax.ShapeDtypeStruct((t, d), F32),
        compiler_params=_params("parallel"),
        name="hy_out_proj",
    )(x, a, b, wa, wb)


DSA_H_WIDTH = 512


def _dsa_proj_body(x_ref, g_ref, wh_ref, wq_ref, qg_ref, kvg_ref, lng_ref, lnb_ref,
                   q_ref, qi_ref, kv_ref, kvt_ref, ki_ref, wt_ref):
    h = _rms(x_ref[...], g_ref[...]).astype(BF16)
    c = _nn(h, wh_ref[...])
    q_lat = _rms(c[:, :DSA_Q_RANK], qg_ref[...]).astype(BF16)
    qq = _nn(q_lat, wq_ref[...])
    nq = DSA_HEADS * DSA_LATENT
    q_ref[...] = (qq[:, :nq] * (DSA_LATENT ** -0.5 * LOG2E)).astype(BF16)
    qi_ref[...] = qq[:, nq:].astype(BF16)
    kv = _rms(c[:, DSA_Q_RANK:DSA_Q_RANK + DSA_LATENT], kvg_ref[...])
    kv_ref[...] = kv.astype(BF16)
    kvt_ref[...] = kv.T.astype(BF16)
    tail = c[:, DSA_Q_RANK + DSA_LATENT:]
    kr = tail[:, :IDX_DIM]
    mu = jnp.mean(kr, axis=-1, keepdims=True)
    var = jnp.mean(jnp.square(kr - mu), axis=-1, keepdims=True)
    ki = (kr - mu) * lax.rsqrt(var + EPS) * lng_ref[...] + lnb_ref[...]
    ki_ref[...] = ki.astype(BF16)
    tail_t = (tail * (IDX_HEADS ** -0.5 * IDX_DIM ** -0.5)).T
    wt_ref[...] = tail_t[IDX_DIM:IDX_DIM + IDX_HEADS, :]


def _dsa_proj(x, g, wh, wq, qg, kvg, lng, lnb, *, tm=512):
    t, d = x.shape
    tm = min(tm, t)
    nq = DSA_HEADS * DSA_LATENT
    ni = IDX_HEADS * IDX_DIM
    full = lambda a: pl.BlockSpec(a.shape, lambda i: (0,) * a.ndim)
    tok = lambda w: pl.BlockSpec((tm, w), lambda i: (i, 0))
    return pl.pallas_call(
        _dsa_proj_body,
        grid=(t // tm,),
        in_specs=[tok(d), full(g), full(wh), full(wq), full(qg), full(kvg), full(lng), full(lnb)],
        out_specs=[tok(nq), tok(ni), tok(DSA_LATENT),
                   pl.BlockSpec((DSA_LATENT, tm), lambda i: (0, i)),
                   tok(IDX_DIM),
                   pl.BlockSpec((IDX_HEADS, tm), lambda i: (0, i))],
        out_shape=[
            jax.ShapeDtypeStruct((t, nq), BF16),
            jax.ShapeDtypeStruct((t, ni), BF16),
            jax.ShapeDtypeStruct((t, DSA_LATENT), BF16),
            jax.ShapeDtypeStruct((DSA_LATENT, t), BF16),
            jax.ShapeDtypeStruct((t, IDX_DIM), BF16),
            jax.ShapeDtypeStruct((IDX_HEADS, t), F32),
        ],
        compiler_params=_params("parallel"),
        name="dsa_proj",
    )(x, g, wh, wq, qg, kvg, lng, lnb)


DSA_TQ = 256
ATTN_HEAD_GROUP = 16


def _dsa_attn_body(q_ref, qi_ref, wt_ref, ki_ref, kv_ref, kvt_ref, bias_ref, x_ref, wuv_ref, wo_ref,
                   o_ref, key_scr, s_scr, acc_scr, m_scr, l_scr, u_scr, *, topk):
    tq = DSA_TQ
    tk = DSA_TQ
    qb = pl.program_id(1)
    krow = lax.broadcasted_iota(I32, (tk, tq), 0)
    qcol = lax.broadcasted_iota(I32, (tk, tq), 1)
    causal = krow <= qcol

    def score_block(kb, diag):
        rows = pl.ds(pl.multiple_of(kb * tk, tk), tk)
        kk = ki_ref[rows, :]
        s = jnp.zeros((tk, tq), F32)
        for h in range(IDX_HEADS):
            z = _nt(kk, qi_ref[:, h * IDX_DIM:(h + 1) * IDX_DIM])
            s = s + jnp.maximum(z, 0.0) * wt_ref[h:h + 1, :]
        if diag:
            s = jnp.where(causal, s, -jnp.inf)
        bits = pltpu.bitcast(s, I32)
        key_scr[rows, :] = bits ^ ((bits >> 31) & 0x7FFFFFFF)

    def score_loop(kb, carry):
        score_block(kb, False)
        return carry

    lax.fori_loop(0, qb, score_loop, 0)
    score_block(qb, True)

    def bit_step(it, thr):
        cand = thr ^ (jnp.int32(1) << (31 - it))

        def count(kb, cnt):
            rows = pl.ds(pl.multiple_of(kb * tk, tk), tk)
            ge = jnp.where(key_scr[rows, :] >= cand, 1, 0)
            return cnt + jnp.sum(ge.reshape(tk // SUBLANES, SUBLANES, tq), axis=0)

        cnt = lax.fori_loop(0, qb + 1, count, jnp.zeros((SUBLANES, tq), I32))
        tot = jnp.sum(cnt, axis=0, keepdims=True)
        return jnp.where(tot >= topk, cand, thr)

    thr = lax.fori_loop(0, 32, bit_step, jnp.full((1, tq), INT_MIN, I32))

    m_scr[...] = jnp.full(m_scr.shape, NEG_BIG, F32)
    l_scr[...] = jnp.zeros_like(l_scr)
    acc_scr[...] = jnp.zeros_like(acc_scr)

    def attn_block(kb, kind):
        rows = pl.ds(pl.multiple_of(kb * tk, tk), tk)
        kvb = kv_ref[rows, :]
        kvtb = kvt_ref[:, rows]
        sel = key_scr[rows, :] >= thr
        if kind == 0:
            sel = jnp.logical_and(sel, causal)
        for g in range(DSA_HEADS // ATTN_HEAD_GROUP):
            heads = range(g * ATTN_HEAD_GROUP, (g + 1) * ATTN_HEAD_GROUP)
            hs = slice(g * ATTN_HEAD_GROUP, (g + 1) * ATTN_HEAD_GROUP)
            blk_max = []
            for h in heads:
                s = _nt(kvb, q_ref[:, h * DSA_LATENT:(h + 1) * DSA_LATENT])
                if kind is not None:
                    s = s + bias_ref[h, kind]
                s = jnp.where(sel, s, NEG_BIG)
                s_scr[h] = s
                part = jnp.max(s.reshape(tk // SUBLANES, SUBLANES, tq), axis=0)
                blk_max.append(jnp.max(part, axis=0, keepdims=True))
            m_old = m_scr[hs, :]
            m_new = jnp.maximum(m_old, jnp.concatenate(blk_max, axis=0))
            alpha = jnp.exp2(m_old - m_new)
            m_scr[hs, :] = m_new
            for i, h in enumerate(heads):
                a_h = alpha[i:i + 1, :]
                p = jnp.exp2(s_scr[h] - m_new[i:i + 1, :])
                l_scr[h] = a_h * l_scr[h] + jnp.sum(p.reshape(tk // SUBLANES, SUBLANES, tq), axis=0)
                acc_scr[h] = acc_scr[h] * a_h + _nn(kvtb, p.astype(BF16))

    def far_loop(kb, carry):
        attn_block(kb, None)
        return carry

    lax.fori_loop(0, jnp.maximum(qb - 1, 0), far_loop, 0)

    @pl.when(qb >= 1)
    def _():
        attn_block(qb - 1, 1)

    attn_block(qb, 0)

    for h in range(DSA_HEADS):
        l_h = jnp.sum(l_scr[h], axis=0, keepdims=True)
        o_t = (acc_scr[h] / l_h).astype(BF16)
        u_scr[h * DSA_VDIM:(h + 1) * DSA_VDIM, :] = _nn(wuv_ref[h], o_t).astype(BF16)
    y_t = _nn(wo_ref[...], u_scr[...])
    o_ref[...] = x_ref[...] + y_t.T


def _dsa_attn(x, q, qi, wt, ki, kv, kvt, bias_t, wuv_t, wo_t, *, bsz, s):
    t, d = x.shape
    tq = DSA_TQ
    nq = s // tq
    topk = min(TOPK_MAX, s // 4)
    full = lambda a: pl.BlockSpec(a.shape, lambda b, i: (0,) * a.ndim)
    return pl.pallas_call(
        functools.partial(_dsa_attn_body, topk=topk),
        grid=(bsz, nq),
        in_specs=[
            pl.BlockSpec((tq, q.shape[1]), lambda b, i: (b * nq + i, 0)),
            pl.BlockSpec((tq, qi.shape[1]), lambda b, i: (b * nq + i, 0)),
            pl.BlockSpec((IDX_HEADS, tq), lambda b, i: (0, b * nq + i)),
            pl.BlockSpec((s, IDX_DIM), lambda b, i: (b, 0)),
            pl.BlockSpec((s, DSA_LATENT), lambda b, i: (b, 0)),
            pl.BlockSpec((DSA_LATENT, s), lambda b, i: (0, b)),
            full(bias_t),
            pl.BlockSpec((tq, d), lambda b, i: (b * nq + i, 0)),
            full(wuv_t), full(wo_t),
        ],
        out_specs=pl.BlockSpec((tq, d), lambda b, i: (b * nq + i, 0)),
        out_shape=jax.ShapeDtypeStruct((t, d), F32),
        scratch_shapes=[
            pltpu.VMEM((s, tq), I32),
            pltpu.VMEM((DSA_HEADS, tq, tq), F32),
            pltpu.VMEM((DSA_HEADS, DSA_LATENT, tq), F32),
            pltpu.VMEM((DSA_HEADS, tq), F32),
            pltpu.VMEM((DSA_HEADS, SUBLANES, tq), F32),
            pltpu.VMEM((DSA_HEADS * DSA_VDIM, tq), BF16),
        ],
        compiler_params=_params("parallel", "arbitrary"),
        name="dsa_attn",
    )(q, qi, wt, ki, kv, kvt, bias_t, x, wuv_t, wo_t)


def _t5_bucket(rel):
    max_exact = REL_BUCKETS // 2
    relf = jnp.maximum(rel, 1).astype(F32)
    large = max_exact + (jnp.log(relf / max_exact) / math.log(REL_MAX_DIST / max_exact)
                         * (REL_BUCKETS - max_exact)).astype(I32)
    large = jnp.minimum(large, REL_BUCKETS - 1)
    return jnp.where(rel < max_exact, rel, large)


def _bias_tiles(rel_bias):
    tq = DSA_TQ
    kk = jnp.arange(tq)[:, None]
    qq = jnp.arange(tq)[None, :]
    rel = jnp.stack([qq - kk, qq - kk + tq])
    table = (rel_bias - rel_bias[REL_BUCKETS - 1][None, :]) * LOG2E
    onehot = (_t5_bucket(jnp.maximum(rel, 0))[..., None] == jnp.arange(REL_BUCKETS)).astype(F32)
    return jnp.einsum("kabn,nh->hkab", onehot, table, precision=lax.Precision.HIGHEST)


def _hy_in_weight(w_in):
    sizes = (GLA_QK, GLA_QK, GLA_V, GLA_V, GLA_GATE_RANK, SSD_INNER, SSD_INNER + 2 * SSD_BC, SSD_HEADS)
    offs = np.cumsum((0,) + sizes)
    q, k, v, r, g_lr, z, xbc, dt = [w_in[:, offs[i]:offs[i + 1]] for i in range(len(sizes))]
    pad = jnp.zeros((w_in.shape[0], HY_SMALL - GLA_GATE_RANK - SSD_HEADS), w_in.dtype)
    return jnp.concatenate([q, k, v, r, z, xbc, g_lr, dt, pad], axis=1).astype(BF16)


def kernel(x, ffn_norm_g, ffn_w_gate, ffn_w_up, ffn_w_down, mix_norm_g, hy_w_in, gla_w_gate2, gla_b_gate, gla_norm_g, ssd_conv_w, ssd_conv_b, ssd_dt_bias, ssd_a_log, ssd_d, ssd_norm_g, hy_w_out, dsa_w_dq, dsa_q_norm_g, dsa_w_uq, dsa_w_dkv, dsa_kv_norm_g, dsa_w_uv, dsa_w_o, idx_w_q, idx_w_k, idx_ln_g, idx_ln_b, idx_w_w, rel_bias, final_norm_g):
    bsz, s, d = x.shape
    t = bsz * s
    xt = x.reshape(t, d)
    row = lambda a: a.reshape(1, -1).astype(F32)

    def ffn(xt, layer, half, final):
        return _ffn(xt, row(ffn_norm_g[layer, half]), ffn_w_gate[layer, half].astype(BF16),
                    ffn_w_up[layer, half].astype(BF16), ffn_w_down[layer, half].astype(BF16),
                    row(final_norm_g), final_norm=final)

    xt = ffn(xt, 0, 0, False)
    p = _in_proj(xt, row(mix_norm_g[0]), _hy_in_weight(hy_w_in[0])).reshape(bsz, s, HY_WIDTH)
    wg2 = jnp.zeros((HY_SMALL, GLA_QK), F32).at[:GLA_GATE_RANK].set(gla_w_gate2[0]).astype(BF16)
    o_gla = _gla(p, wg2, row(gla_b_gate[0]), row(gla_norm_g[0]))
    cw, cb = ssd_conv_w[0].astype(F32), ssd_conv_b[0].astype(F32)
    lane_pad = lambda v: jnp.zeros((1, HY_SMALL), F32).at[0, HY_DT_LANE:HY_DT_LANE + SSD_HEADS].set(v)
    y_ssd = _ssd(p, cw[:, :SSD_INNER], row(cb[:SSD_INNER]), cw[:, SSD_INNER:], row(cb[SSD_INNER:]),
                 lane_pad(ssd_dt_bias[0].astype(F32)), lane_pad(-jnp.exp(ssd_a_log[0].astype(F32))),
                 row(jnp.repeat(ssd_d[0], SSD_HEADDIM)), row(ssd_norm_g[0]))
    w_out = hy_w_out[0].astype(BF16)
    xt = _out_proj(xt, o_gla.reshape(t, GLA_V), y_ssd.reshape(t, SSD_INNER), w_out[:GLA_V], w_out[GLA_V:])
    xt = ffn(xt, 0, 1, False)

    xt = ffn(xt, 1, 0, False)
    wh = jnp.concatenate(
        [dsa_w_dq[0], dsa_w_dkv[0], idx_w_k[0], idx_w_w[0],
         jnp.zeros((d, DSA_H_WIDTH - DSA_Q_RANK - DSA_LATENT - IDX_DIM - IDX_HEADS), F32)], axis=1).astype(BF16)
    wq = jnp.concatenate([dsa_w_uq[0], idx_w_q[0]], axis=1).astype(BF16)
    q, qi, kv, kvt, ki, wt = _dsa_proj(xt, row(mix_norm_g[1]), wh, wq, row(dsa_q_norm_g[0]),
                                       row(dsa_kv_norm_g[0]), row(idx_ln_g[0]), row(idx_ln_b[0]))
    wuv_t = jnp.transpose(dsa_w_uv[0], (0, 2, 1)).astype(BF16)
    wo_t = dsa_w_o[0].T.astype(BF16)
    xt = _dsa_attn(xt, q, qi, wt, ki, kv, kvt, _bias_tiles(rel_bias.astype(F32)), wuv_t, wo_t, bsz=bsz, s=s)
    xt = ffn(xt, 1, 1, True)
    return xt.reshape(bsz, s, d)
```

```python
import functools
import math

import numpy as np
import jax
import jax.numpy as jnp
from jax import lax
from jax.experimental import pallas as pl
from jax.experimental.pallas import tpu as pltpu

F32 = jnp.float32
BF16 = jnp.bfloat16
I32 = jnp.int32

EPS = 1e-6
D_MODEL = 1024
D_FF = 2816
GLA_HEADS = 4
GLA_DK = 128
GLA_DV = 256
GLA_GATE_RANK = 16
GLA_GATE_TAU = 16.0
GLA_CHUNK = 128
GLA_QK = GLA_HEADS * GLA_DK
GLA_V = GLA_HEADS * GLA_DV
SSD_HEADS = 16
SSD_HEADDIM = 64
SSD_STATE = 128
SSD_GROUPS = 2
SSD_CONV = 4
SSD_CHUNK = 128
SSD_INNER = SSD_HEADS * SSD_HEADDIM
SSD_BC = SSD_GROUPS * SSD_STATE
SSD_HEADS_PER_GROUP = SSD_HEADS // SSD_GROUPS
DSA_HEADS = 16
DSA_Q_RANK = 256
DSA_LATENT = 128
DSA_VDIM = 64
IDX_HEADS = 16
IDX_DIM = 64
TOPK_MAX = 256
REL_BUCKETS = 32
REL_MAX_DIST = 128

LANES = 128
SUBLANES = 8
VMEM_LIMIT = 56 * 1024 * 1024

HY_SMALL = LANES
HY_MAIN = 2 * GLA_QK + 2 * GLA_V + SSD_INNER + SSD_INNER + 2 * SSD_BC
HY_WIDTH = HY_MAIN + HY_SMALL
HY_DT_LANE = GLA_GATE_RANK

NEG_BIG = -1e30
LOG2E = math.log2(math.e)
INT_MIN = -2 ** 31


def _nt(a, b):
    return lax.dot_general(a, b, (((1,), (1,)), ((), ())), preferred_element_type=F32)


def _tn(a, b):
    return lax.dot_general(a, b, (((0,), (0,)), ((), ())), preferred_element_type=F32)


def _nn(a, b):
    return jnp.dot(a, b, preferred_element_type=F32)


def _silu(a):
    return a / (1.0 + jnp.exp(-a))


def _softplus(a):
    return jnp.maximum(a, 0.0) + jnp.log1p(jnp.exp(-jnp.abs(a)))


def _rms(x, g):
    return x * lax.rsqrt(jnp.mean(x * x, axis=-1, keepdims=True) + EPS) * g


def _params(*sem):
    return pltpu.CompilerParams(dimension_semantics=sem, vmem_limit_bytes=VMEM_LIMIT)


def _ffn_body(x_ref, g_ref, wg_ref, wu_ref, wd_ref, fg_ref, o_ref, *, final_norm):
    x = x_ref[...]
    h = _rms(x, g_ref[...]).astype(BF16)
    a = _nn(h, wg_ref[...])
    b = _nn(h, wu_ref[...])
    t = (_silu(a) * b).astype(BF16)
    y = x + 0.5 * _nn(t, wd_ref[...])
    if final_norm:
        y = _rms(y, fg_ref[...])
    o_ref[...] = y


def _resident(shape):
    return pl.BlockSpec(shape, lambda *_: (0,) * len(shape), pipeline_mode=pl.Buffered(1))


def _ffn(x, g, wg, wu, wd, fg, *, final_norm, tm=512):
    t, d = x.shape
    tm = min(tm, t)
    return pl.pallas_call(
        functools.partial(_ffn_body, final_norm=final_norm),
        grid=(t // tm,),
        in_specs=[
            pl.BlockSpec((tm, d), lambda i: (i, 0)),
            _resident(g.shape), _resident(wg.shape), _resident(wu.shape), _resident(wd.shape),
            _resident(fg.shape),
        ],
        out_specs=pl.BlockSpec((tm, d), lambda i: (i, 0)),
        out_shape=jax.ShapeDtypeStruct((t, d), F32),
        compiler_params=_params("parallel"),
        name="ffn",
    )(x, g, wg, wu, wd, fg)


def _in_proj_body(x_ref, g_ref, w_ref, o_ref, sm_ref):
    h = _rms(x_ref[...], g_ref[...]).astype(BF16)
    o_ref[...] = _nn(h, w_ref[:, :HY_MAIN]).astype(BF16)
    sm_ref[...] = _nn(h, w_ref[:, HY_MAIN:])


def _in_proj(x, g, w, *, tm=512):
    t, d = x.shape
    tm = min(tm, t)
    return pl.pallas_call(
        _in_proj_body,
        grid=(t // tm,),
        in_specs=[pl.BlockSpec((tm, d), lambda i: (i, 0)), _resident(g.shape), _resident(w.shape)],
        out_specs=[pl.BlockSpec((tm, HY_MAIN), lambda i: (i, 0)), pl.BlockSpec((tm, HY_SMALL), lambda i: (i, 0))],
        out_shape=[jax.ShapeDtypeStruct((t, HY_MAIN), BF16), jax.ShapeDtypeStruct((t, HY_SMALL), F32)],
        compiler_params=_params("parallel"),
        name="hy_in_proj",
    )(x, g, w)


def _gla_body(q_ref, k_ref, v_ref, r_ref, sm_ref, wg2_ref, bg_ref, ng_ref, o_ref, st_scr, *, n_chunks):
    @pl.when(pl.program_id(1) == 0)
    def _():
        st_scr[...] = jnp.zeros_like(st_scr)

    c = GLA_CHUNK
    row = lax.broadcasted_iota(I32, (c, c), 0)
    col = lax.broadcasted_iota(I32, (c, c), 1)
    tril = row >= col
    tri_f = tril.astype(F32)
    scale = GLA_DK ** -0.5

    def chunk(ci, carry):
        rows = pl.ds(pl.multiple_of(ci * c, c), c)
        sm = sm_ref[0, rows, :].astype(BF16)
        gl = _nn(sm, wg2_ref[...]) + bg_ref[...]
        g = -_softplus(-gl) * (1.0 / GLA_GATE_TAU)
        b = jnp.dot(tri_f, g, preferred_element_type=F32, precision=lax.Precision.HIGHEST)
        b_last = b[c - 1:c, :]
        b_mid = b[c // 2 - 1:c // 2, :]
        e_b = jnp.exp(b)
        e_qm = jnp.exp(b - b_mid)
        e_km = jnp.exp(b_mid - b)
        e_kd = jnp.exp(b_last - b)
        e_last = jnp.exp(b_last)
        q = q_ref[0, rows, :].astype(F32) * scale
        k = k_ref[0, rows, :].astype(F32)
        for h in range(GLA_HEADS):
            ks = slice(h * GLA_DK, (h + 1) * GLA_DK)
            vs = slice(h * GLA_DV, (h + 1) * GLA_DV)
            q_inter = (q[:, ks] * e_b[:, ks]).astype(BF16)
            q_intra = (q[:, ks] * e_qm[:, ks]).astype(BF16)
            kt = (k[:, ks] * e_km[:, ks]).astype(BF16)
            kd = (k[:, ks] * e_kd[:, ks]).astype(BF16)
            v = v_ref[0, rows, vs]
            st = st_scr[h]
            attn = jnp.where(tril, _nt(q_intra, kt), 0.0).astype(BF16)
            o = _nt(q_inter, st.astype(BF16)) + _nn(attn, v)
            st_scr[h] = st * e_last[:, ks] + _tn(v, kd)
            o = _rms(o, ng_ref[:, vs])
            o_ref[0, rows, vs] = (o * _silu(r_ref[0, rows, vs].astype(F32))).astype(BF16)
        return carry

    lax.fori_loop(0, n_chunks, chunk, 0, unroll=2)


def _gla(p, sm, wg2, bg, ng, *, cs=512):
    bsz, s, _ = p.shape
    cs = min(cs, s)
    return pl.pallas_call(
        functools.partial(_gla_body, n_chunks=cs // GLA_CHUNK),
        grid=(bsz, s // cs),
        in_specs=[
            pl.BlockSpec((1, cs, GLA_QK), lambda b, i: (b, i, 0)),
            pl.BlockSpec((1, cs, GLA_QK), lambda b, i: (b, i, 1)),
            pl.BlockSpec((1, cs, GLA_V), lambda b, i: (b, i, 1)),
            pl.BlockSpec((1, cs, GLA_V), lambda b, i: (b, i, 2)),
            pl.BlockSpec((1, cs, HY_SMALL), lambda b, i: (b, i, 0)),
            _resident(wg2.shape), _resident(bg.shape), _resident(ng.shape),
        ],
        out_specs=pl.BlockSpec((1, cs, GLA_V), lambda b, i: (b, i, 0)),
        out_shape=jax.ShapeDtypeStruct((bsz, s, GLA_V), BF16),
        scratch_shapes=[pltpu.VMEM((GLA_HEADS, GLA_DV, GLA_DK), F32)],
        compiler_params=_params("parallel", "arbitrary"),
        name="gla",
    )(p, p, p, p, sm, wg2, bg, ng)


def _ssd_body(z_ref, xs_ref, bc_ref, sm_ref, cwx_ref, cbx_ref, cwb_ref, cbb_ref, dtb_ref, a_ref, d_ref,
              ng_ref, ex_ref, exb_ref, o_ref, xpad, bpad, st_scr, y_scr, *, cs, n_chunks):
    first = pl.program_id(1) == 0
    halo = SUBLANES

    @pl.when(first)
    def _():
        st_scr[...] = jnp.zeros_like(st_scr)
        xpad[0:halo, :] = jnp.zeros((halo, SSD_INNER), F32)
        bpad[0:halo, :] = jnp.zeros((halo, 2 * SSD_BC), F32)

    @pl.when(jnp.logical_not(first))
    def _():
        xpad[0:halo, :] = xpad[cs:cs + halo, :]
        bpad[0:halo, :] = bpad[cs:cs + halo, :]

    xpad[halo:halo + cs, :] = xs_ref[0].astype(F32)
    bpad[halo:halo + cs, :] = bc_ref[0].astype(F32)

    c = SSD_CHUNK
    row = lax.broadcasted_iota(I32, (c, c), 0)
    col = lax.broadcasted_iota(I32, (c, c), 1)
    tril = row >= col
    tri_f = tril.astype(F32)
    lo_half = lax.broadcasted_iota(I32, (c, 2 * SSD_HEADDIM), 1) < SSD_HEADDIM
    gw = SSD_INNER // SSD_GROUPS

    def conv(pad_ref, w_ref, b_ref, start):
        win = pad_ref[pl.ds(start, c + halo), :]
        acc = b_ref[...]
        for kk in range(SSD_CONV):
            off = halo - (SSD_CONV - 1) + kk
            acc = acc + w_ref[kk:kk + 1, :] * win[off:off + c, :]
        return _silu(acc)

    def chunk(ci, carry):
        start = pl.multiple_of(ci * c, c)
        rows = pl.ds(start, c)
        xc = conv(xpad, cwx_ref, cbx_ref, start)
        bcc = conv(bpad, cwb_ref, cbb_ref, start)
        dt = _softplus(sm_ref[0, rows, :] + dtb_ref[...])
        da = dt * a_ref[...]
        acum = jnp.dot(tri_f, da, preferred_element_type=F32, precision=lax.Precision.HIGHEST)
        acum_t = acum.T
        a_last = acum[c - 1:c, :]
        fac = jnp.concatenate([dt, jnp.exp(acum), jnp.exp(a_last - acum)], axis=0).astype(BF16)
        fac_x = _nn(fac, exb_ref[...])
        dt_x, ea_x, w_x = fac_x[0:c], fac_x[c:2 * c], fac_x[2 * c:3 * c]
        dec_x = jnp.dot(jnp.broadcast_to(jnp.exp(a_last), (SUBLANES, HY_SMALL)), ex_ref[...],
                        preferred_element_type=F32, precision=lax.Precision.HIGHEST)[0:1]
        xdt = xc * dt_x
        xdt_b = xdt.astype(BF16)
        xdtw_b = (xdt * w_x).astype(BF16)
        for grp in range(SSD_GROUPS):
            gs = slice(grp * gw, (grp + 1) * gw)
            bm = bcc[:, grp * SSD_STATE:(grp + 1) * SSD_STATE].astype(BF16)
            cm = bcc[:, SSD_BC + grp * SSD_STATE:SSD_BC + (grp + 1) * SSD_STATE].astype(BF16)
            scores = _nt(cm, bm)
            st = st_scr[:, gs]
            y_inter = _nn(cm, st.astype(BF16)) * ea_x[:, gs]
            st_scr[:, gs] = st * dec_x[:, gs] + _tn(bm, xdtw_b[:, gs])
            for pair in range(SSD_HEADS_PER_GROUP // 2):
                h0 = grp * SSD_HEADS_PER_GROUP + 2 * pair
                ps = slice(h0 * SSD_HEADDIM, (h0 + 2) * SSD_HEADDIM)
                lhs = []
                for h in (h0, h0 + 1):
                    ln = HY_DT_LANE + h
                    seg = jnp.where(tril, jnp.exp(acum[:, ln:ln + 1] - acum_t[ln:ln + 1, :]), 0.0)
                    lhs.append((scores * seg).astype(BF16))
                slab = xdt_b[:, ps]
                zero = jnp.zeros_like(slab)
                rhs = jnp.concatenate([jnp.where(lo_half, slab, zero), jnp.where(lo_half, zero, slab)], axis=0)
                y_pair = _nn(jnp.concatenate(lhs, axis=1), rhs)
                y_scr[:, ps] = (y_pair + y_inter[:, 2 * pair * SSD_HEADDIM:(2 * pair + 2) * SSD_HEADDIM]
                                + d_ref[:, ps] * xc[:, ps])
        y = y_scr[...] * _silu(z_ref[0, rows, :].astype(F32))
        for grp in range(SSD_GROUPS):
            gs = slice(grp * gw, (grp + 1) * gw)
            o_ref[0, rows, gs] = _rms(y[:, gs], ng_ref[:, gs]).astype(BF16)
        return carry

    lax.fori_loop(0, n_chunks, chunk, 0)


def _head_expander():
    e = np.zeros((HY_SMALL, SSD_INNER), np.float32)
    for h in range(SSD_HEADS):
        e[HY_DT_LANE + h, h * SSD_HEADDIM:(h + 1) * SSD_HEADDIM] = 1.0
    return jnp.asarray(e)


def _ssd(p, sm, cwx, cbx, cwb, cbb, dtb, a_pad, d_full, ng, *, cs=512):
    bsz, s, _ = p.shape
    cs = min(cs, s)
    ex = _head_expander()
    consts = (cwx, cbx, cwb, cbb, dtb, a_pad, d_full, ng, ex, ex.astype(BF16))
    return pl.pallas_call(
        functools.partial(_ssd_body, cs=cs, n_chunks=cs // SSD_CHUNK),
        grid=(bsz, s // cs),
        in_specs=[
            pl.BlockSpec((1, cs, SSD_INNER), lambda b, i: (b, i, 3)),
            pl.BlockSpec((1, cs, SSD_INNER), lambda b, i: (b, i, 4)),
            pl.BlockSpec((1, cs, 2 * SSD_BC), lambda b, i: (b, i, 10)),
            pl.BlockSpec((1, cs, HY_SMALL), lambda b, i: (b, i, 0)),
        ] + [_resident(a.shape) for a in consts],
        out_specs=pl.BlockSpec((1, cs, SSD_INNER), lambda b, i: (b, i, 0)),
        out_shape=jax.ShapeDtypeStruct((bsz, s, SSD_INNER), BF16),
        scratch_shapes=[
            pltpu.VMEM((cs + SUBLANES, SSD_INNER), F32),
            pltpu.VMEM((cs + SUBLANES, 2 * SSD_BC), F32),
            pltpu.VMEM((SSD_STATE, SSD_INNER), F32),
            pltpu.VMEM((SSD_CHUNK, SSD_INNER), F32),
        ],
        compiler_params=_params("parallel", "arbitrary"),
        name="ssd",
    )(p, p, p, sm, *consts)


def _out_proj_body(x_ref, a_ref, b_ref, wa_ref, wb_ref, o_ref):
    o_ref[...] = x_ref[...] + _nn(a_ref[...], wa_ref[...]) + _nn(b_ref[...], wb_ref[...])


def _out_proj(x, a, b, wa, wb, *, tm=512):
    t, d = x.shape
    tm = min(tm, t)
    tok = lambda w: pl.BlockSpec((tm, w), lambda i: (i, 0))
    return pl.pallas_call(
        _out_proj_body,
        grid=(t // tm,),
        in_specs=[tok(d), tok(a.shape[1]), tok(b.shape[1]),
                  _resident(wa.shape), _resident(wb.shape)],
        out_specs=tok(d),
        out_shape=jax.ShapeDtypeStruct((t, d), F32),
        compiler_params=_params("parallel"),
        name="hy_out_proj",
    )(x, a, b, wa, wb)


DSA_H_WIDTH = 512
DSA_LAT_EXT = DSA_LATENT + 16


def _dsa_proj_body(x_ref, g_ref, wh_ref, wq_ref, qg_ref, kvg_ref, lng_ref, lnb_ref,
                   q_ref, qi_ref, kv_ref, kvt_ref, ki_ref, wt_ref):
    h = _rms(x_ref[...], g_ref[...]).astype(BF16)
    c = _nn(h, wh_ref[...])
    q_lat = _rms(c[:, :DSA_Q_RANK], qg_ref[...]).astype(BF16)
    qq = _nn(q_lat, wq_ref[...])
    nq = DSA_HEADS * DSA_LATENT
    q_ref[...] = (qq[:, :nq] * (DSA_LATENT ** -0.5 * LOG2E)).astype(BF16)
    qi_ref[...] = qq[:, nq:].astype(BF16)
    kv = _rms(c[:, DSA_Q_RANK:DSA_Q_RANK + DSA_LATENT], kvg_ref[...])
    kv_ref[...] = kv.astype(BF16)
    kvt_ref[0:DSA_LATENT, :] = kv.T.astype(BF16)
    ones_row = lax.broadcasted_iota(I32, (DSA_LAT_EXT - DSA_LATENT, kv.shape[0]), 0) == 0
    kvt_ref[DSA_LATENT:, :] = ones_row.astype(F32).astype(BF16)
    tail = c[:, DSA_Q_RANK + DSA_LATENT:]
    kr = tail[:, :IDX_DIM]
    mu = jnp.mean(kr, axis=-1, keepdims=True)
    var = jnp.mean(jnp.square(kr - mu), axis=-1, keepdims=True)
    ki = (kr - mu) * lax.rsqrt(var + EPS) * lng_ref[...] + lnb_ref[...]
    ki_ref[...] = ki.astype(BF16)
    tail_t = (tail * (IDX_HEADS ** -0.5 * IDX_DIM ** -0.5)).T
    wt_ref[...] = tail_t[IDX_DIM:IDX_DIM + IDX_HEADS, :]


def _dsa_proj(x, g, wh, wq, qg, kvg, lng, lnb, *, tm=512):
    t, d = x.shape
    tm = min(tm, t)
    nq = DSA_HEADS * DSA_LATENT
    ni = IDX_HEADS * IDX_DIM
    full = lambda a: _resident(a.shape)
    tok = lambda w: pl.BlockSpec((tm, w), lambda i: (i, 0))
    return pl.pallas_call(
        _dsa_proj_body,
        grid=(t // tm,),
        in_specs=[tok(d), full(g), full(wh), full(wq), full(qg), full(kvg), full(lng), full(lnb)],
        out_specs=[tok(nq), tok(ni), tok(DSA_LATENT),
                   pl.BlockSpec((DSA_LAT_EXT, tm), lambda i: (0, i)),
                   tok(IDX_DIM),
                   pl.BlockSpec((IDX_HEADS, tm), lambda i: (0, i))],
        out_shape=[
            jax.ShapeDtypeStruct((t, nq), BF16),
            jax.ShapeDtypeStruct((t, ni), BF16),
            jax.ShapeDtypeStruct((t, DSA_LATENT), BF16),
            jax.ShapeDtypeStruct((DSA_LAT_EXT, t), BF16),
            jax.ShapeDtypeStruct((t, IDX_DIM), BF16),
            jax.ShapeDtypeStruct((IDX_HEADS, t), F32),
        ],
        compiler_params=_params("parallel"),
        name="dsa_proj",
    )(x, g, wh, wq, qg, kvg, lng, lnb)


DSA_TQ = 256
PART_ROWS = 4 * SUBLANES
ATTN_HEAD_GROUP = 16


def _dsa_attn_body(q_ref, qi_ref, wt_ref, ki_ref, kv_ref, kvt_ref, bias_ref, x_ref, wuv_ref, wo_ref,
                   o_ref, key_scr, s_scr, acc_scr, m_scr, u_scr, *, topk):
    tq = DSA_TQ
    tk = DSA_TQ
    qb = pl.program_id(1)
    krow = lax.broadcasted_iota(I32, (tk, tq), 0)
    qcol = lax.broadcasted_iota(I32, (tk, tq), 1)
    causal = krow <= qcol

    def score_block(kb, diag):
        rows = pl.ds(pl.multiple_of(kb * tk, tk), tk)
        kk = ki_ref[rows, :]
        s = jnp.zeros((tk, tq), F32)
        for h in range(IDX_HEADS):
            z = _nt(kk, qi_ref[:, h * IDX_DIM:(h + 1) * IDX_DIM])
            s = s + jnp.maximum(z, 0.0) * wt_ref[h:h + 1, :]
        if diag:
            s = jnp.where(causal, s, -jnp.inf)
        bits = pltpu.bitcast(s, I32)
        key_scr[rows, :] = bits ^ ((bits >> 31) & 0x7FFFFFFF)

    def score_loop(kb, carry):
        score_block(kb, False)
        return carry

    lax.fori_loop(0, qb, score_loop, 0)
    score_block(qb, True)
    key_scr[pl.ds(pl.multiple_of((qb + 1) * tk, tk), tk), :] = jnp.full((tk, tq), INT_MIN, I32)

    def bit_step(it, thr):
        cand = thr ^ (jnp.int32(1) << (31 - it))

        def count(kp, cnt):
            rows = pl.ds(pl.multiple_of(kp * (2 * tk), 2 * tk), 2 * tk)
            ge = jnp.where(key_scr[rows, :] >= cand, 1, 0)
            return cnt + jnp.sum(ge.reshape(2 * tk // PART_ROWS, PART_ROWS, tq), axis=0)

        cnt = lax.fori_loop(0, (qb + 2) // 2, count, jnp.zeros((PART_ROWS, tq), I32))
        tot = jnp.sum(cnt, axis=0, keepdims=True)
        return jnp.where(tot >= topk, cand, thr)

    thr = lax.fori_loop(0, 32, bit_step, jnp.full((1, tq), INT_MIN, I32))

    m_scr[...] = jnp.full(m_scr.shape, NEG_BIG, F32)
    acc_scr[...] = jnp.zeros_like(acc_scr)

    def attn_block(kb, kind):
        rows = pl.ds(pl.multiple_of(kb * tk, tk), tk)
        kvb = kv_ref[rows, :]
        kvtb = kvt_ref[:, rows]
        sel = key_scr[rows, :] >= thr
        if kind == 0:
            sel = jnp.logical_and(sel, causal)
        for g in range(DSA_HEADS // ATTN_HEAD_GROUP):
            heads = range(g * ATTN_HEAD_GROUP, (g + 1) * ATTN_HEAD_GROUP)
            hs = slice(g * ATTN_HEAD_GROUP, (g + 1) * ATTN_HEAD_GROUP)
            blk_max = []
            for h in heads:
                s = _nt(kvb, q_ref[:, h * DSA_LATENT:(h + 1) * DSA_LATENT])
                if kind is not None:
                    s = s + bias_ref[h, kind]
                s = jnp.where(sel, s, NEG_BIG)
                s_scr[h] = s
                part = jnp.max(s.reshape(tk // PART_ROWS, PART_ROWS, tq), axis=0)
                blk_max.append(jnp.max(part, axis=0, keepdims=True))
            m_old = m_scr[hs, :]
            m_new = jnp.maximum(m_old, jnp.concatenate(blk_max, axis=0))
            alpha = jnp.exp2(m_old - m_new)
            m_scr[hs, :] = m_new
            for i, h in enumerate(heads):
                a_h = alpha[i:i + 1, :]
                p = jnp.exp2(s_scr[h] - m_new[i:i + 1, :])
                acc_scr[h] = acc_scr[h] * a_h + _nn(kvtb, p.astype(BF16))

    def far_loop(kb, carry):
        attn_block(kb, None)
        return carry

    lax.fori_loop(0, jnp.maximum(qb - 1, 0), far_loop, 0)

    @pl.when(qb >= 1)
    def _():
        attn_block(qb - 1, 1)

    attn_block(qb, 0)

    for h in range(DSA_HEADS):
        acc = acc_scr[h]
        o_t = (acc[0:DSA_LATENT] / acc[DSA_LATENT:DSA_LATENT + 1]).astype(BF16)
        u_scr[h * DSA_VDIM:(h + 1) * DSA_VDIM, :] = _nn(wuv_ref[h], o_t).astype(BF16)
    y_t = _nn(wo_ref[...], u_scr[...])
    o_ref[...] = x_ref[...] + y_t.T


def _dsa_attn(x, q, qi, wt, ki, kv, kvt, bias_t, wuv_t, wo_t, *, bsz, s):
    t, d = x.shape
    tq = DSA_TQ
    nq = s // tq
    topk = min(TOPK_MAX, s // 4)
    full = lambda a: _resident(a.shape)
    return pl.pallas_call(
        functools.partial(_dsa_attn_body, topk=topk),
        grid=(bsz, nq),
        in_specs=[
            pl.BlockSpec((tq, q.shape[1]), lambda b, i: (b * nq + i, 0)),
            pl.BlockSpec((tq, qi.shape[1]), lambda b, i: (b * nq + i, 0)),
            pl.BlockSpec((IDX_HEADS, tq), lambda b, i: (0, b * nq + i)),
            pl.BlockSpec((s, IDX_DIM), lambda b, i: (b, 0)),
            pl.BlockSpec((s, DSA_LATENT), lambda b, i: (b, 0)),
            pl.BlockSpec((DSA_LAT_EXT, s), lambda b, i: (0, b)),
            full(bias_t),
            pl.BlockSpec((tq, d), lambda b, i: (b * nq + i, 0)),
            full(wuv_t), full(wo_t),
        ],
        out_specs=pl.BlockSpec((tq, d), lambda b, i: (b * nq + i, 0)),
        out_shape=jax.ShapeDtypeStruct((t, d), F32),
        scratch_shapes=[
            pltpu.VMEM((s + tq, tq), I32),
            pltpu.VMEM((DSA_HEADS, tq, tq), F32),
            pltpu.VMEM((DSA_HEADS, DSA_LAT_EXT, tq), F32),
            pltpu.VMEM((DSA_HEADS, tq), F32),
            pltpu.VMEM((DSA_HEADS * DSA_VDIM, tq), BF16),
        ],
        compiler_params=_params("parallel", "arbitrary"),
        name="dsa_attn",
    )(q, qi, wt, ki, kv, kvt, bias_t, x, wuv_t, wo_t)


def _t5_bucket(rel):
    max_exact = REL_BUCKETS // 2
    relf = jnp.maximum(rel, 1).astype(F32)
    large = max_exact + (jnp.log(relf / max_exact) / math.log(REL_MAX_DIST / max_exact)
                         * (REL_BUCKETS - max_exact)).astype(I32)
    large = jnp.minimum(large, REL_BUCKETS - 1)
    return jnp.where(rel < max_exact, rel, large)


def _bias_tiles(rel_bias):
    tq = DSA_TQ
    kk = jnp.arange(tq)[:, None]
    qq = jnp.arange(tq)[None, :]
    rel = jnp.stack([qq - kk, qq - kk + tq])
    table = (rel_bias - rel_bias[REL_BUCKETS - 1][None, :]) * LOG2E
    onehot = (_t5_bucket(jnp.maximum(rel, 0))[..., None] == jnp.arange(REL_BUCKETS)).astype(F32)
    return jnp.einsum("kabn,nh->hkab", onehot, table, precision=lax.Precision.HIGHEST)


def _hy_in_weight(w_in):
    sizes = (GLA_QK, GLA_QK, GLA_V, GLA_V, GLA_GATE_RANK, SSD_INNER, SSD_INNER + 2 * SSD_BC, SSD_HEADS)
    offs = np.cumsum((0,) + sizes)
    q, k, v, r, g_lr, z, xbc, dt = [w_in[:, offs[i]:offs[i + 1]] for i in range(len(sizes))]
    pad = jnp.zeros((w_in.shape[0], HY_SMALL - GLA_GATE_RANK - SSD_HEADS), w_in.dtype)
    return jnp.concatenate([q, k, v, r, z, xbc, g_lr, dt, pad], axis=1).astype(BF16)


def kernel(x, ffn_norm_g, ffn_w_gate, ffn_w_up, ffn_w_down, mix_norm_g, hy_w_in, gla_w_gate2, gla_b_gate, gla_norm_g, ssd_conv_w, ssd_conv_b, ssd_dt_bias, ssd_a_log, ssd_d, ssd_norm_g, hy_w_out, dsa_w_dq, dsa_q_norm_g, dsa_w_uq, dsa_w_dkv, dsa_kv_norm_g, dsa_w_uv, dsa_w_o, idx_w_q, idx_w_k, idx_ln_g, idx_ln_b, idx_w_w, rel_bias, final_norm_g):
    bsz, s, d = x.shape
    t = bsz * s
    xt = x.reshape(t, d)
    row = lambda a: a.reshape(1, -1).astype(F32)

    def ffn(xt, layer, half, final):
        return _ffn(xt, row(ffn_norm_g[layer, half]), ffn_w_gate[layer, half].astype(BF16),
                    ffn_w_up[layer, half].astype(BF16), ffn_w_down[layer, half].astype(BF16),
                    row(final_norm_g), final_norm=final)

    xt = ffn(xt, 0, 0, False)
    p, sm = _in_proj(xt, row(mix_norm_g[0]), _hy_in_weight(hy_w_in[0]))
    p = p.reshape(bsz, s, HY_MAIN)
    sm = sm.reshape(bsz, s, HY_SMALL)
    wg2 = jnp.zeros((HY_SMALL, GLA_QK), F32).at[:GLA_GATE_RANK].set(gla_w_gate2[0]).astype(BF16)
    o_gla = _gla(p, sm, wg2, row(gla_b_gate[0]), row(gla_norm_g[0]))
    cw, cb = ssd_conv_w[0].astype(F32), ssd_conv_b[0].astype(F32)
    lane_pad = lambda v: jnp.zeros((1, HY_SMALL), F32).at[0, HY_DT_LANE:HY_DT_LANE + SSD_HEADS].set(v)
    y_ssd = _ssd(p, sm, cw[:, :SSD_INNER], row(cb[:SSD_INNER]), cw[:, SSD_INNER:], row(cb[SSD_INNER:]),
                 lane_pad(ssd_dt_bias[0].astype(F32)), lane_pad(-jnp.exp(ssd_a_log[0].astype(F32))),
                 row(jnp.repeat(ssd_d[0], SSD_HEADDIM)), row(ssd_norm_g[0]))
    w_out = hy_w_out[0].astype(BF16)
    xt = _out_proj(xt, o_gla.reshape(t, GLA_V), y_ssd.reshape(t, SSD_INNER), w_out[:GLA_V], w_out[GLA_V:])
    xt = ffn(xt, 0, 1, False)

    xt = ffn(xt, 1, 0, False)
    wh = jnp.concatenate(
        [dsa_w_dq[0], dsa_w_dkv[0], idx_w_k[0], idx_w_w[0],
         jnp.zeros((d, DSA_H_WIDTH - DSA_Q_RANK - DSA_LATENT - IDX_DIM - IDX_HEADS), F32)], axis=1).astype(BF16)
    wq = jnp.concatenate([dsa_w_uq[0], idx_w_q[0]], axis=1).astype(BF16)
    q, qi, kv, kvt, ki, wt = _dsa_proj(xt, row(mix_norm_g[1]), wh, wq, row(dsa_q_norm_g[0]),
                                       row(dsa_kv_norm_g[0]), row(idx_ln_g[0]), row(idx_ln_b[0]))
    wuv_t = jnp.transpose(dsa_w_uv[0], (0, 2, 1)).astype(BF16)
    wo_t = dsa_w_o[0].T.astype(BF16)
    xt = _dsa_attn(xt, q, qi, wt, ki, kv, kvt, _bias_tiles(rel_bias.astype(F32)), wuv_t, wo_t, bsz=bsz, s=s)
    xt = ffn(xt, 1, 1, True)
    return xt.reshape(bsz, s, d)
```

```python
import functools
import math

import numpy as np
import jax
import jax.numpy as jnp
from jax import lax
from jax.experimental import pallas as pl
from jax.experimental.pallas import tpu as pltpu

F32 = jnp.float32
BF16 = jnp.bfloat16
I32 = jnp.int32

EPS = 1e-6
D_MODEL = 1024
D_FF = 2816
GLA_HEADS = 4
GLA_DK = 128
GLA_DV = 256
GLA_GATE_RANK = 16
GLA_GATE_TAU = 16.0
GLA_CHUNK = 128
GLA_QK = GLA_HEADS * GLA_DK
GLA_V = GLA_HEADS * GLA_DV
SSD_HEADS = 16
SSD_HEADDIM = 64
SSD_STATE = 128
SSD_GROUPS = 2
SSD_CONV = 4
SSD_CHUNK = 128
SSD_INNER = SSD_HEADS * SSD_HEADDIM
SSD_BC = SSD_GROUPS * SSD_STATE
SSD_HEADS_PER_GROUP = SSD_HEADS // SSD_GROUPS
DSA_HEADS = 16
DSA_Q_RANK = 256
DSA_LATENT = 128
DSA_VDIM = 64
IDX_HEADS = 16
IDX_DIM = 64
TOPK_MAX = 256
REL_BUCKETS = 32
REL_MAX_DIST = 128

LANES = 128
SUBLANES = 8
VMEM_LIMIT = 56 * 1024 * 1024

HY_SMALL = LANES
HY_MAIN = 2 * GLA_QK + 2 * GLA_V + SSD_INNER + SSD_INNER + 2 * SSD_BC
HY_WIDTH = HY_MAIN + HY_SMALL
HY_DT_LANE = GLA_GATE_RANK

NEG_BIG = -1e30
LOG2E = math.log2(math.e)
INT_MIN = -2 ** 31


def _nt(a, b):
    return lax.dot_general(a, b, (((1,), (1,)), ((), ())), preferred_element_type=F32)


def _tn(a, b):
    return lax.dot_general(a, b, (((0,), (0,)), ((), ())), preferred_element_type=F32)


def _nn(a, b):
    return jnp.dot(a, b, preferred_element_type=F32)


def _silu(a):
    return a / (1.0 + jnp.exp(-a))


def _softplus(a):
    return jnp.maximum(a, 0.0) + jnp.log1p(jnp.exp(-jnp.abs(a)))


def _rms(x, g):
    return x * lax.rsqrt(jnp.mean(x * x, axis=-1, keepdims=True) + EPS) * g


def _params(*sem):
    return pltpu.CompilerParams(dimension_semantics=sem, vmem_limit_bytes=VMEM_LIMIT)


def _ffn_body(x_ref, g_ref, wg_ref, wu_ref, wd_ref, fg_ref, o_ref, *, final_norm):
    x = x_ref[...]
    h = _rms(x, g_ref[...]).astype(BF16)
    a = _nn(h, wg_ref[...])
    b = _nn(h, wu_ref[...])
    t = (_silu(a) * b).astype(BF16)
    y = x + 0.5 * _nn(t, wd_ref[...])
    if final_norm:
        y = _rms(y, fg_ref[...])
    o_ref[...] = y


def _resident(shape):
    return pl.BlockSpec(shape, lambda *_: (0,) * len(shape), pipeline_mode=pl.Buffered(1))


def _ffn(x, g, wg, wu, wd, fg, *, final_norm, tm=512):
    t, d = x.shape
    tm = min(tm, t)
    return pl.pallas_call(
        functools.partial(_ffn_body, final_norm=final_norm),
        grid=(t // tm,),
        in_specs=[
            pl.BlockSpec((tm, d), lambda i: (i, 0)),
            _resident(g.shape), _resident(wg.shape), _resident(wu.shape), _resident(wd.shape),
            _resident(fg.shape),
        ],
        out_specs=pl.BlockSpec((tm, d), lambda i: (i, 0)),
        out_shape=jax.ShapeDtypeStruct((t, d), F32),
        compiler_params=_params("parallel"),
        name="ffn",
    )(x, g, wg, wu, wd, fg)


def _in_proj_body(x_ref, g_ref, w_ref, o_ref, sm_ref):
    h = _rms(x_ref[...], g_ref[...]).astype(BF16)
    o_ref[...] = _nn(h, w_ref[:, :HY_MAIN]).astype(BF16)
    sm_ref[...] = _nn(h, w_ref[:, HY_MAIN:])


def _in_proj(x, g, w, *, tm=512):
    t, d = x.shape
    tm = min(tm, t)
    return pl.pallas_call(
        _in_proj_body,
        grid=(t // tm,),
        in_specs=[pl.BlockSpec((tm, d), lambda i: (i, 0)), _resident(g.shape), _resident(w.shape)],
        out_specs=[pl.BlockSpec((tm, HY_MAIN), lambda i: (i, 0)), pl.BlockSpec((tm, HY_SMALL), lambda i: (i, 0))],
        out_shape=[jax.ShapeDtypeStruct((t, HY_MAIN), BF16), jax.ShapeDtypeStruct((t, HY_SMALL), F32)],
        compiler_params=_params("parallel"),
        name="hy_in_proj",
    )(x, g, w)


def _gla_body(q_ref, k_ref, v_ref, r_ref, sm_ref, wg2_ref, bg_ref, ng_ref, o_ref, st_scr, *, n_chunks):
    @pl.when(pl.program_id(1) == 0)
    def _():
        st_scr[...] = jnp.zeros_like(st_scr)

    c = GLA_CHUNK
    row = lax.broadcasted_iota(I32, (c, c), 0)
    col = lax.broadcasted_iota(I32, (c, c), 1)
    tril = row >= col
    tri_f = tril.astype(F32)
    scale = GLA_DK ** -0.5

    def chunk(ci, carry):
        rows = pl.ds(pl.multiple_of(ci * c, c), c)
        sm = sm_ref[0, rows, :].astype(BF16)
        gl = _nn(sm, wg2_ref[...]) + bg_ref[...]
        g = -_softplus(-gl) * (1.0 / GLA_GATE_TAU)
        b = jnp.dot(tri_f, g, preferred_element_type=F32, precision=lax.Precision.HIGHEST)
        b_last = b[c - 1:c, :]
        b_mid = b[c // 2 - 1:c // 2, :]
        e_b = jnp.exp(b)
        e_qm = jnp.exp(b - b_mid)
        e_km = jnp.exp(b_mid - b)
        e_kd = jnp.exp(b_last - b)
        e_last = jnp.exp(b_last)
        q = q_ref[0, rows, :].astype(F32) * scale
        k = k_ref[0, rows, :].astype(F32)
        for h in range(GLA_HEADS):
            ks = slice(h * GLA_DK, (h + 1) * GLA_DK)
            vs = slice(h * GLA_DV, (h + 1) * GLA_DV)
            q_inter = (q[:, ks] * e_b[:, ks]).astype(BF16)
            q_intra = (q[:, ks] * e_qm[:, ks]).astype(BF16)
            kt = (k[:, ks] * e_km[:, ks]).astype(BF16)
            kd = (k[:, ks] * e_kd[:, ks]).astype(BF16)
            v = v_ref[0, rows, vs]
            st = st_scr[h]
            attn = jnp.where(tril, _nt(q_intra, kt), 0.0).astype(BF16)
            o = _nt(q_inter, st.astype(BF16)) + _nn(attn, v)
            st_scr[h] = st * e_last[:, ks] + _tn(v, kd)
            o = _rms(o, ng_ref[:, vs])
            o_ref[0, rows, vs] = (o * _silu(r_ref[0, rows, vs].astype(F32))).astype(BF16)
        return carry

    lax.fori_loop(0, n_chunks, chunk, 0, unroll=2)


def _gla(p, sm, wg2, bg, ng, *, cs=512):
    bsz, s, _ = p.shape
    cs = min(cs, s)
    return pl.pallas_call(
        functools.partial(_gla_body, n_chunks=cs // GLA_CHUNK),
        grid=(bsz, s // cs),
        in_specs=[
            pl.BlockSpec((1, cs, GLA_QK), lambda b, i: (b, i, 0)),
            pl.BlockSpec((1, cs, GLA_QK), lambda b, i: (b, i, 1)),
            pl.BlockSpec((1, cs, GLA_V), lambda b, i: (b, i, 1)),
            pl.BlockSpec((1, cs, GLA_V), lambda b, i: (b, i, 2)),
            pl.BlockSpec((1, cs, HY_SMALL), lambda b, i: (b, i, 0)),
            _resident(wg2.shape), _resident(bg.shape), _resident(ng.shape),
        ],
        out_specs=pl.BlockSpec((1, cs, GLA_V), lambda b, i: (b, i, 0)),
        out_shape=jax.ShapeDtypeStruct((bsz, s, GLA_V), BF16),
        scratch_shapes=[pltpu.VMEM((GLA_HEADS, GLA_DV, GLA_DK), F32)],
        compiler_params=_params("parallel", "arbitrary"),
        name="gla",
    )(p, p, p, p, sm, wg2, bg, ng)


def _ssd_body(z_ref, xs_ref, bc_ref, sm_ref, cwx_ref, cbx_ref, cwb_ref, cbb_ref, dtb_ref, a_ref, d_ref,
              ng_ref, ex_ref, exb_ref, o_ref, xpad, bpad, st_scr, y_scr, *, cs, n_chunks):
    first = pl.program_id(1) == 0
    halo = SUBLANES

    @pl.when(first)
    def _():
        st_scr[...] = jnp.zeros_like(st_scr)
        xpad[0:halo, :] = jnp.zeros((halo, SSD_INNER), F32)
        bpad[0:halo, :] = jnp.zeros((halo, 2 * SSD_BC), F32)

    @pl.when(jnp.logical_not(first))
    def _():
        xpad[0:halo, :] = xpad[cs:cs + halo, :]
        bpad[0:halo, :] = bpad[cs:cs + halo, :]

    xpad[halo:halo + cs, :] = xs_ref[0].astype(F32)
    bpad[halo:halo + cs, :] = bc_ref[0].astype(F32)

    c = SSD_CHUNK
    row = lax.broadcasted_iota(I32, (c, c), 0)
    col = lax.broadcasted_iota(I32, (c, c), 1)
    tril = row >= col
    tri_f = tril.astype(F32)
    lo_half = lax.broadcasted_iota(I32, (c, 2 * SSD_HEADDIM), 1) < SSD_HEADDIM
    gw = SSD_INNER // SSD_GROUPS

    def conv(pad_ref, w_ref, b_ref, start):
        win = pad_ref[pl.ds(start, c + halo), :]
        acc = b_ref[...]
        for kk in range(SSD_CONV):
            off = halo - (SSD_CONV - 1) + kk
            acc = acc + w_ref[kk:kk + 1, :] * win[off:off + c, :]
        return _silu(acc)

    def chunk(ci, carry):
        start = pl.multiple_of(ci * c, c)
        rows = pl.ds(start, c)
        xc = conv(xpad, cwx_ref, cbx_ref, start)
        bcc = conv(bpad, cwb_ref, cbb_ref, start)
        dt = _softplus(sm_ref[0, rows, :] + dtb_ref[...])
        da = dt * a_ref[...]
        acum = jnp.dot(tri_f, da, preferred_element_type=F32, precision=lax.Precision.HIGHEST)
        acum_t = acum.T
        a_last = acum[c - 1:c, :]
        fac = jnp.concatenate([dt, jnp.exp(acum), jnp.exp(a_last - acum)], axis=0).astype(BF16)
        fac_x = _nn(fac, exb_ref[...])
        dt_x, ea_x, w_x = fac_x[0:c], fac_x[c:2 * c], fac_x[2 * c:3 * c]
        dec_x = jnp.dot(jnp.broadcast_to(jnp.exp(a_last), (SUBLANES, HY_SMALL)), ex_ref[...],
                        preferred_element_type=F32, precision=lax.Precision.HIGHEST)[0:1]
        xdt = xc * dt_x
        xdt_b = xdt.astype(BF16)
        xdtw_b = (xdt * w_x).astype(BF16)
        for grp in range(SSD_GROUPS):
            gs = slice(grp * gw, (grp + 1) * gw)
            bm = bcc[:, grp * SSD_STATE:(grp + 1) * SSD_STATE].astype(BF16)
            cm = bcc[:, SSD_BC + grp * SSD_STATE:SSD_BC + (grp + 1) * SSD_STATE].astype(BF16)
            scores = _nt(cm, bm)
            st = st_scr[:, gs]
            y_inter = _nn(cm, st.astype(BF16)) * ea_x[:, gs]
            st_scr[:, gs] = st * dec_x[:, gs] + _tn(bm, xdtw_b[:, gs])
            for pair in range(SSD_HEADS_PER_GROUP // 2):
                h0 = grp * SSD_HEADS_PER_GROUP + 2 * pair
                ps = slice(h0 * SSD_HEADDIM, (h0 + 2) * SSD_HEADDIM)
                lhs = []
                for h in (h0, h0 + 1):
                    ln = HY_DT_LANE + h
                    seg = jnp.where(tril, jnp.exp(acum[:, ln:ln + 1] - acum_t[ln:ln + 1, :]), 0.0)
                    lhs.append((scores * seg).astype(BF16))
                slab = xdt_b[:, ps]
                zero = jnp.zeros_like(slab)
                rhs = jnp.concatenate([jnp.where(lo_half, slab, zero), jnp.where(lo_half, zero, slab)], axis=0)
                y_pair = _nn(jnp.concatenate(lhs, axis=1), rhs)
                y_scr[:, ps] = (y_pair + y_inter[:, 2 * pair * SSD_HEADDIM:(2 * pair + 2) * SSD_HEADDIM]
                                + d_ref[:, ps] * xc[:, ps])
        y = y_scr[...] * _silu(z_ref[0, rows, :].astype(F32))
        for grp in range(SSD_GROUPS):
            gs = slice(grp * gw, (grp + 1) * gw)
            o_ref[0, rows, gs] = _rms(y[:, gs], ng_ref[:, gs]).astype(BF16)
        return carry

    lax.fori_loop(0, n_chunks, chunk, 0)


def _head_expander():
    e = np.zeros((HY_SMALL, SSD_INNER), np.float32)
    for h in range(SSD_HEADS):
        e[HY_DT_LANE + h, h * SSD_HEADDIM:(h + 1) * SSD_HEADDIM] = 1.0
    return jnp.asarray(e)


def _ssd(p, sm, cwx, cbx, cwb, cbb, dtb, a_pad, d_full, ng, *, cs=512):
    bsz, s, _ = p.shape
    cs = min(cs, s)
    ex = _head_expander()
    consts = (cwx, cbx, cwb, cbb, dtb, a_pad, d_full, ng, ex, ex.astype(BF16))
    return pl.pallas_call(
        functools.partial(_ssd_body, cs=cs, n_chunks=cs // SSD_CHUNK),
        grid=(bsz, s // cs),
        in_specs=[
            pl.BlockSpec((1, cs, SSD_INNER), lambda b, i: (b, i, 3)),
            pl.BlockSpec((1, cs, SSD_INNER), lambda b, i: (b, i, 4)),
            pl.BlockSpec((1, cs, 2 * SSD_BC), lambda b, i: (b, i, 10)),
            pl.BlockSpec((1, cs, HY_SMALL), lambda b, i: (b, i, 0)),
        ] + [_resident(a.shape) for a in consts],
        out_specs=pl.BlockSpec((1, cs, SSD_INNER), lambda b, i: (b, i, 0)),
        out_shape=jax.ShapeDtypeStruct((bsz, s, SSD_INNER), BF16),
        scratch_shapes=[
            pltpu.VMEM((cs + SUBLANES, SSD_INNER), F32),
            pltpu.VMEM((cs + SUBLANES, 2 * SSD_BC), F32),
            pltpu.VMEM((SSD_STATE, SSD_INNER), F32),
            pltpu.VMEM((SSD_CHUNK, SSD_INNER), F32),
        ],
        compiler_params=_params("parallel", "arbitrary"),
        name="ssd",
    )(p, p, p, sm, *consts)


def _out_proj_body(x_ref, a_ref, b_ref, wa_ref, wb_ref, o_ref):
    o_ref[...] = x_ref[...] + _nn(a_ref[...], wa_ref[...]) + _nn(b_ref[...], wb_ref[...])


def _out_proj(x, a, b, wa, wb, *, tm=512):
    t, d = x.shape
    tm = min(tm, t)
    tok = lambda w: pl.BlockSpec((tm, w), lambda i: (i, 0))
    return pl.pallas_call(
        _out_proj_body,
        grid=(t // tm,),
        in_specs=[tok(d), tok(a.shape[1]), tok(b.shape[1]),
                  _resident(wa.shape), _resident(wb.shape)],
        out_specs=tok(d),
        out_shape=jax.ShapeDtypeStruct((t, d), F32),
        compiler_params=_params("parallel"),
        name="hy_out_proj",
    )(x, a, b, wa, wb)


DSA_H_WIDTH = 512
DSA_LAT_EXT = DSA_LATENT + 16


def _dsa_proj_body(x_ref, g_ref, wh_ref, wq_ref, qg_ref, kvg_ref, lng_ref, lnb_ref,
                   q_ref, qi_ref, kv_ref, kvt_ref, ki_ref, wt_ref):
    h = _rms(x_ref[...], g_ref[...]).astype(BF16)
    c = _nn(h, wh_ref[...])
    q_lat = _rms(c[:, :DSA_Q_RANK], qg_ref[...]).astype(BF16)
    qq = _nn(q_lat, wq_ref[...])
    nq = DSA_HEADS * DSA_LATENT
    q_ref[...] = (qq[:, :nq] * (DSA_LATENT ** -0.5 * LOG2E)).astype(BF16)
    qi_ref[...] = qq[:, nq:].astype(BF16)
    kv = _rms(c[:, DSA_Q_RANK:DSA_Q_RANK + DSA_LATENT], kvg_ref[...])
    kv_ref[...] = kv.astype(BF16)
    kvt_ref[0:DSA_LATENT, :] = kv.T.astype(BF16)
    ones_row = lax.broadcasted_iota(I32, (DSA_LAT_EXT - DSA_LATENT, kv.shape[0]), 0) == 0
    kvt_ref[DSA_LATENT:, :] = ones_row.astype(F32).astype(BF16)
    tail = c[:, DSA_Q_RANK + DSA_LATENT:]
    kr = tail[:, :IDX_DIM]
    mu = jnp.mean(kr, axis=-1, keepdims=True)
    var = jnp.mean(jnp.square(kr - mu), axis=-1, keepdims=True)
    ki = (kr - mu) * lax.rsqrt(var + EPS) * lng_ref[...] + lnb_ref[...]
    ki_ref[...] = ki.astype(BF16)
    tail_t = (tail * (IDX_HEADS ** -0.5 * IDX_DIM ** -0.5)).T
    wt_ref[...] = tail_t[IDX_DIM:IDX_DIM + IDX_HEADS, :]


def _dsa_proj(x, g, wh, wq, qg, kvg, lng, lnb, *, tm=512):
    t, d = x.shape
    tm = min(tm, t)
    nq = DSA_HEADS * DSA_LATENT
    ni = IDX_HEADS * IDX_DIM
    full = lambda a: _resident(a.shape)
    tok = lambda w: pl.BlockSpec((tm, w), lambda i: (i, 0))
    return pl.pallas_call(
        _dsa_proj_body,
        grid=(t // tm,),
        in_specs=[tok(d), full(g), full(wh), full(wq), full(qg), full(kvg), full(lng), full(lnb)],
        out_specs=[tok(nq), tok(ni), tok(DSA_LATENT),
                   pl.BlockSpec((DSA_LAT_EXT, tm), lambda i: (0, i)),
                   tok(IDX_DIM),
                   pl.BlockSpec((IDX_HEADS, tm), lambda i: (0, i))],
        out_shape=[
            jax.ShapeDtypeStruct((t, nq), BF16),
            jax.ShapeDtypeStruct((t, ni), BF16),
            jax.ShapeDtypeStruct((t, DSA_LATENT), BF16),
            jax.ShapeDtypeStruct((DSA_LAT_EXT, t), BF16),
            jax.ShapeDtypeStruct((t, IDX_DIM), BF16),
            jax.ShapeDtypeStruct((IDX_HEADS, t), F32),
        ],
        compiler_params=_params("parallel"),
        name="dsa_proj",
    )(x, g, wh, wq, qg, kvg, lng, lnb)


DSA_TQ = 256
PART_ROWS = 4 * SUBLANES


def _dsa_attn_body(q_ref, qi_ref, wt_ref, ki_ref, kv_ref, kvt_ref, bias_ref, x_ref, wuv_ref, wo_ref,
                   o_ref, key_scr, s_scr, acc_scr, m_scr, bm_scr, al_scr, mb_scr, u_scr, *, topk):
    tq = DSA_TQ
    tk = DSA_TQ
    qb = pl.program_id(1)
    krow = lax.broadcasted_iota(I32, (tk, tq), 0)
    qcol = lax.broadcasted_iota(I32, (tk, tq), 1)
    causal = krow <= qcol

    def score_block(row0, n_rows, diag):
        rows = pl.ds(pl.multiple_of(row0, tk), n_rows)
        kk = ki_ref[rows, :]
        s = jnp.zeros((n_rows, tq), F32)
        for h in range(IDX_HEADS):
            z = _nt(kk, qi_ref[:, h * IDX_DIM:(h + 1) * IDX_DIM])
            s = s + jnp.maximum(z, 0.0) * wt_ref[h:h + 1, :]
        if diag:
            s = jnp.where(causal, s, -jnp.inf)
        bits = pltpu.bitcast(s, I32)
        key_scr[rows, :] = bits ^ ((bits >> 31) & 0x7FFFFFFF)

    def score_loop(kp, carry):
        score_block(kp * (2 * tk), 2 * tk, False)
        return carry

    lax.fori_loop(0, qb // 2, score_loop, 0)

    @pl.when(qb % 2 == 1)
    def _():
        score_block((qb - 1) * tk, tk, False)

    score_block(qb * tk, tk, True)
    key_scr[pl.ds(pl.multiple_of((qb + 1) * tk, tk), tk), :] = jnp.full((tk, tq), INT_MIN, I32)

    def bit_step(it, thr):
        cand = thr ^ (jnp.int32(1) << (31 - it))

        def count(kp, cnt):
            rows = pl.ds(pl.multiple_of(kp * (2 * tk), 2 * tk), 2 * tk)
            ge = jnp.where(key_scr[rows, :] >= cand, 1, 0)
            return cnt + jnp.sum(ge.reshape(2 * tk // PART_ROWS, PART_ROWS, tq), axis=0)

        cnt = lax.fori_loop(0, (qb + 2) // 2, count, jnp.zeros((PART_ROWS, tq), I32))
        tot = jnp.sum(cnt, axis=0, keepdims=True)
        return jnp.where(tot >= topk, cand, thr)

    thr = lax.fori_loop(0, 32, bit_step, jnp.full((1, tq), INT_MIN, I32))

    m_scr[...] = jnp.full(m_scr.shape, NEG_BIG, F32)
    acc_scr[...] = jnp.zeros_like(acc_scr)
    key_minus_query = krow - qcol

    def block_rows(kb):
        return pl.ds(pl.multiple_of(kb * tk, tk), tk)

    def tiles(x):
        return x.reshape(x.shape[0] // SUBLANES, SUBLANES, tq)

    def block_max(s):
        part = jnp.max(s.reshape(tk // PART_ROWS, PART_ROWS, tq), axis=0)
        m8 = jnp.max(tiles(part), axis=0)
        for shift in (4, 2, 1):
            m8 = jnp.maximum(m8, pltpu.roll(m8, shift, axis=0))
        return m8

    def attn_step(kb, cur, with_logits, with_probs):
        prev = 1 - cur
        if with_logits:
            kvb = kv_ref[block_rows(kb), :]
            sel = jnp.logical_and(key_scr[block_rows(kb), :] >= thr, key_minus_query <= (qb - kb) * tk)
            cap = jnp.where(sel, jnp.inf, NEG_BIG)
        if with_probs:
            kvtb = kvt_ref[:, block_rows(kb - 1)]
        for h in range(DSA_HEADS):
            if with_logits:
                s = _nt(kvb, q_ref[:, h * DSA_LATENT:(h + 1) * DSA_LATENT])
                s = jnp.minimum(s, cap)
                s_scr[cur, h] = tiles(s)
                bm_scr[h] = block_max(s)
            if with_probs:
                p = jnp.exp2(s_scr[prev, h] - mb_scr[prev, h][None]).reshape(tk, tq)
                acc_scr[h] = acc_scr[h] * al_scr[prev, h][None] + tiles(_nn(kvtb, p.astype(BF16)))
        if with_logits:
            @pl.when(kb >= qb - 1)
            def _():
                for h in range(DSA_HEADS):
                    s = s_scr[cur, h].reshape(tk, tq) + bias_ref[h, qb - kb]
                    s_scr[cur, h] = tiles(s)
                    bm_scr[h] = block_max(s)

            m_old = m_scr[...]
            m_new = jnp.maximum(m_old, bm_scr[...])
            al_scr[cur] = jnp.exp2(m_old - m_new)
            mb_scr[cur] = m_new
            m_scr[...] = m_new

    attn_step(0, 0, True, False)

    def attn_pair(j, carry):
        attn_step(2 * j + 1, 1, True, True)
        attn_step(2 * j + 2, 0, True, True)
        return carry

    lax.fori_loop(0, qb // 2, attn_pair, 0)

    @pl.when(qb % 2 == 1)
    def _():
        attn_step(qb, 1, True, True)
        attn_step(qb + 1, 0, False, True)

    @pl.when(qb % 2 == 0)
    def _():
        attn_step(qb + 1, 1, False, True)

    for h in range(DSA_HEADS):
        acc = acc_scr[h].reshape(DSA_LAT_EXT, tq)
        o_t = (acc[0:DSA_LATENT] / acc[DSA_LATENT:DSA_LATENT + 1]).astype(BF16)
        u_scr[h * DSA_VDIM:(h + 1) * DSA_VDIM, :] = _nn(wuv_ref[h], o_t).astype(BF16)
    y_t = _nn(wo_ref[...], u_scr[...])
    o_ref[...] = x_ref[...] + y_t.T


def _dsa_attn(x, q, qi, wt, ki, kv, kvt, bias_t, wuv_t, wo_t, *, bsz, s):
    t, d = x.shape
    tq = DSA_TQ
    nq = s // tq
    topk = min(TOPK_MAX, s // 4)
    full = lambda a: _resident(a.shape)
    return pl.pallas_call(
        functools.partial(_dsa_attn_body, topk=topk),
        grid=(bsz, nq),
        in_specs=[
            pl.BlockSpec((tq, q.shape[1]), lambda b, i: (b * nq + i, 0)),
            pl.BlockSpec((tq, qi.shape[1]), lambda b, i: (b * nq + i, 0)),
            pl.BlockSpec((IDX_HEADS, tq), lambda b, i: (0, b * nq + i)),
            pl.BlockSpec((s, IDX_DIM), lambda b, i: (b, 0)),
            pl.BlockSpec((s, DSA_LATENT), lambda b, i: (b, 0)),
            pl.BlockSpec((DSA_LAT_EXT, s), lambda b, i: (0, b)),
            full(bias_t),
            pl.BlockSpec((tq, d), lambda b, i: (b * nq + i, 0)),
            full(wuv_t), full(wo_t),
        ],
        out_specs=pl.BlockSpec((tq, d), lambda b, i: (b * nq + i, 0)),
        out_shape=jax.ShapeDtypeStruct((t, d), F32),
        scratch_shapes=[
            pltpu.VMEM((s + tq, tq), I32),
            pltpu.VMEM((2, DSA_HEADS, tq // SUBLANES, SUBLANES, tq), F32),
            pltpu.VMEM((DSA_HEADS, DSA_LAT_EXT // SUBLANES, SUBLANES, tq), F32),
            pltpu.VMEM((DSA_HEADS, SUBLANES, tq), F32),
            pltpu.VMEM((DSA_HEADS, SUBLANES, tq), F32),
            pltpu.VMEM((2, DSA_HEADS, SUBLANES, tq), F32),
            pltpu.VMEM((2, DSA_HEADS, SUBLANES, tq), F32),
            pltpu.VMEM((DSA_HEADS * DSA_VDIM, tq), BF16),
        ],
        compiler_params=_params("parallel", "arbitrary"),
        name="dsa_attn",
    )(q, qi, wt, ki, kv, kvt, bias_t, x, wuv_t, wo_t)


def _t5_bucket(rel):
    max_exact = REL_BUCKETS // 2
    relf = jnp.maximum(rel, 1).astype(F32)
    large = max_exact + (jnp.log(relf / max_exact) / math.log(REL_MAX_DIST / max_exact)
                         * (REL_BUCKETS - max_exact)).astype(I32)
    large = jnp.minimum(large, REL_BUCKETS - 1)
    return jnp.where(rel < max_exact, rel, large)


def _bias_tiles(rel_bias):
    tq = DSA_TQ
    kk = jnp.arange(tq)[:, None]
    qq = jnp.arange(tq)[None, :]
    rel = jnp.stack([qq - kk, qq - kk + tq])
    table = (rel_bias - rel_bias[REL_BUCKETS - 1][None, :]) * LOG2E
    onehot = (_t5_bucket(jnp.maximum(rel, 0))[..., None] == jnp.arange(REL_BUCKETS)).astype(F32)
    return jnp.einsum("kabn,nh->hkab", onehot, table, precision=lax.Precision.HIGHEST)


def _hy_in_weight(w_in):
    sizes = (GLA_QK, GLA_QK, GLA_V, GLA_V, GLA_GATE_RANK, SSD_INNER, SSD_INNER + 2 * SSD_BC, SSD_HEADS)
    offs = np.cumsum((0,) + sizes)
    q, k, v, r, g_lr, z, xbc, dt = [w_in[:, offs[i]:offs[i + 1]] for i in range(len(sizes))]
    pad = jnp.zeros((w_in.shape[0], HY_SMALL - GLA_GATE_RANK - SSD_HEADS), w_in.dtype)
    return jnp.concatenate([q, k, v, r, z, xbc, g_lr, dt, pad], axis=1).astype(BF16)


def kernel(x, ffn_norm_g, ffn_w_gate, ffn_w_up, ffn_w_down, mix_norm_g, hy_w_in, gla_w_gate2, gla_b_gate, gla_norm_g, ssd_conv_w, ssd_conv_b, ssd_dt_bias, ssd_a_log, ssd_d, ssd_norm_g, hy_w_out, dsa_w_dq, dsa_q_norm_g, dsa_w_uq, dsa_w_dkv, dsa_kv_norm_g, dsa_w_uv, dsa_w_o, idx_w_q, idx_w_k, idx_ln_g, idx_ln_b, idx_w_w, rel_bias, final_norm_g):
    bsz, s, d = x.shape
    t = bsz * s
    xt = x.reshape(t, d)
    row = lambda a: a.reshape(1, -1).astype(F32)

    def ffn(xt, layer, half, final):
        return _ffn(xt, row(ffn_norm_g[layer, half]), ffn_w_gate[layer, half].astype(BF16),
                    ffn_w_up[layer, half].astype(BF16), ffn_w_down[layer, half].astype(BF16),
                    row(final_norm_g), final_norm=final)

    xt = ffn(xt, 0, 0, False)
    p, sm = _in_proj(xt, row(mix_norm_g[0]), _hy_in_weight(hy_w_in[0]))
    p = p.reshape(bsz, s, HY_MAIN)
    sm = sm.reshape(bsz, s, HY_SMALL)
    wg2 = jnp.zeros((HY_SMALL, GLA_QK), F32).at[:GLA_GATE_RANK].set(gla_w_gate2[0]).astype(BF16)
    o_gla = _gla(p, sm, wg2, row(gla_b_gate[0]), row(gla_norm_g[0]))
    cw, cb = ssd_conv_w[0].astype(F32), ssd_conv_b[0].astype(F32)
    lane_pad = lambda v: jnp.zeros((1, HY_SMALL), F32).at[0, HY_DT_LANE:HY_DT_LANE + SSD_HEADS].set(v)
    y_ssd = _ssd(p, sm, cw[:, :SSD_INNER], row(cb[:SSD_INNER]), cw[:, SSD_INNER:], row(cb[SSD_INNER:]),
                 lane_pad(ssd_dt_bias[0].astype(F32)), lane_pad(-jnp.exp(ssd_a_log[0].astype(F32))),
                 row(jnp.repeat(ssd_d[0], SSD_HEADDIM)), row(ssd_norm_g[0]))
    w_out = hy_w_out[0].astype(BF16)
    xt = _out_proj(xt, o_gla.reshape(t, GLA_V), y_ssd.reshape(t, SSD_INNER), w_out[:GLA_V], w_out[GLA_V:])
    xt = ffn(xt, 0, 1, False)

    xt = ffn(xt, 1, 0, False)
    wh = jnp.concatenate(
        [dsa_w_dq[0], dsa_w_dkv[0], idx_w_k[0], idx_w_w[0],
         jnp.zeros((d, DSA_H_WIDTH - DSA_Q_RANK - DSA_LATENT - IDX_DIM - IDX_HEADS), F32)], axis=1).astype(BF16)
    wq = jnp.concatenate([dsa_w_uq[0], idx_w_q[0]], axis=1).astype(BF16)
    q, qi, kv, kvt, ki, wt = _dsa_proj(xt, row(mix_norm_g[1]), wh, wq, row(dsa_q_norm_g[0]),
                                       row(dsa_kv_norm_g[0]), row(idx_ln_g[0]), row(idx_ln_b[0]))
    wuv_t = jnp.transpose(dsa_w_uv[0], (0, 2, 1)).astype(BF16)
    wo_t = dsa_w_o[0].T.astype(BF16)
    xt = _dsa_attn(xt, q, qi, wt, ki, kv, kvt, _bias_tiles(rel_bias.astype(F32)), wuv_t, wo_t, bsz=bsz, s=s)
    xt = ffn(xt, 1, 1, True)
    return xt.reshape(bsz, s, d)
```

```python
import functools
import math

import numpy as np
import jax
import jax.numpy as jnp
from jax import lax
from jax.experimental import pallas as pl
from jax.experimental.pallas import tpu as pltpu

F32 = jnp.float32
BF16 = jnp.bfloat16
I32 = jnp.int32

EPS = 1e-6
D_MODEL = 1024
D_FF = 2816
GLA_HEADS = 4
GLA_DK = 128
GLA_DV = 256
GLA_GATE_RANK = 16
GLA_GATE_TAU = 16.0
GLA_CHUNK = 128
GLA_QK = GLA_HEADS * GLA_DK
GLA_V = GLA_HEADS * GLA_DV
SSD_HEADS = 16
SSD_HEADDIM = 64
SSD_STATE = 128
SSD_GROUPS = 2
SSD_CONV = 4
SSD_CHUNK = 128
SSD_INNER = SSD_HEADS * SSD_HEADDIM
SSD_BC = SSD_GROUPS * SSD_STATE
SSD_HEADS_PER_GROUP = SSD_HEADS // SSD_GROUPS
DSA_HEADS = 16
DSA_Q_RANK = 256
DSA_LATENT = 128
DSA_VDIM = 64
IDX_HEADS = 16
IDX_DIM = 64
TOPK_MAX = 256
REL_BUCKETS = 32
REL_MAX_DIST = 128

LANES = 128
SUBLANES = 8
BF16_SUBLANES = 16
VMEM_LIMIT = 56 * 1024 * 1024

HY_SMALL = LANES
HY_MAIN = 2 * GLA_QK + 2 * GLA_V + SSD_INNER + SSD_INNER + 2 * SSD_BC
HY_WIDTH = HY_MAIN + HY_SMALL
HY_DT_LANE = GLA_GATE_RANK

NEG_BIG = -1e30
LOG2E = math.log2(math.e)
INT_MIN = -2 ** 31


def _nt(a, b):
    return lax.dot_general(a, b, (((1,), (1,)), ((), ())), preferred_element_type=F32)


def _tn(a, b):
    return lax.dot_general(a, b, (((0,), (0,)), ((), ())), preferred_element_type=F32)


def _nn(a, b):
    return jnp.dot(a, b, preferred_element_type=F32)


def _silu(a):
    return a / (1.0 + jnp.exp(-a))


def _softplus(a):
    return jnp.maximum(a, 0.0) + jnp.log1p(jnp.exp(-jnp.abs(a)))


def _rms(x, g):
    return x * lax.rsqrt(jnp.mean(x * x, axis=-1, keepdims=True) + EPS) * g


def _params(*sem):
    return pltpu.CompilerParams(dimension_semantics=sem, vmem_limit_bytes=VMEM_LIMIT)


def _ffn_body(x_ref, g_ref, wg_ref, wu_ref, wd_ref, fg_ref, o_ref, *, final_norm):
    x = x_ref[...]
    h = _rms(x, g_ref[...]).astype(BF16)
    a = _nn(h, wg_ref[...])
    b = _nn(h, wu_ref[...])
    t = (_silu(a) * b).astype(BF16)
    y = x + 0.5 * _nn(t, wd_ref[...])
    if final_norm:
        y = _rms(y, fg_ref[...])
    o_ref[...] = y


def _resident(shape):
    return pl.BlockSpec(shape, lambda *_: (0,) * len(shape), pipeline_mode=pl.Buffered(1))


def _ffn(x, g, wg, wu, wd, fg, *, layer, half, final_norm, tm=512):
    t, d = x.shape
    tm = min(tm, t)
    pick = lambda w: pl.BlockSpec((None, None) + w.shape[2:], lambda i: (layer, half, 0, 0),
                                  pipeline_mode=pl.Buffered(1))
    return pl.pallas_call(
        functools.partial(_ffn_body, final_norm=final_norm),
        grid=(t // tm,),
        in_specs=[
            pl.BlockSpec((tm, d), lambda i: (i, 0)),
            _resident(g.shape), pick(wg), pick(wu), pick(wd),
            _resident(fg.shape),
        ],
        out_specs=pl.BlockSpec((tm, d), lambda i: (i, 0)),
        out_shape=jax.ShapeDtypeStruct((t, d), F32),
        compiler_params=_params("parallel"),
        name="ffn",
    )(x, g, wg, wu, wd, fg)


def _in_proj_body(x_ref, g_ref, w_ref, o_ref, sm_ref):
    h = _rms(x_ref[...], g_ref[...]).astype(BF16)
    o_ref[...] = _nn(h, w_ref[:, :HY_MAIN]).astype(BF16)
    sm_ref[...] = _nn(h, w_ref[:, HY_MAIN:])


def _in_proj(x, g, w, *, tm=512):
    t, d = x.shape
    tm = min(tm, t)
    return pl.pallas_call(
        _in_proj_body,
        grid=(t // tm,),
        in_specs=[pl.BlockSpec((tm, d), lambda i: (i, 0)), _resident(g.shape), _resident(w.shape)],
        out_specs=[pl.BlockSpec((tm, HY_MAIN), lambda i: (i, 0)), pl.BlockSpec((tm, HY_SMALL), lambda i: (i, 0))],
        out_shape=[jax.ShapeDtypeStruct((t, HY_MAIN), BF16), jax.ShapeDtypeStruct((t, HY_SMALL), F32)],
        compiler_params=_params("parallel"),
        name="hy_in_proj",
    )(x, g, w)


def _gla_body(q_ref, k_ref, v_ref, r_ref, sm_ref, wg2_ref, bg_ref, ng_ref, o_ref, st_scr, *, n_chunks):
    @pl.when(pl.program_id(1) == 0)
    def _():
        st_scr[...] = jnp.zeros_like(st_scr)

    c = GLA_CHUNK
    row = lax.broadcasted_iota(I32, (c, c), 0)
    col = lax.broadcasted_iota(I32, (c, c), 1)
    tril = row >= col
    tri_f = tril.astype(F32)
    scale = GLA_DK ** -0.5

    def chunk(ci, carry):
        rows = pl.ds(pl.multiple_of(ci * c, c), c)
        sm = sm_ref[0, rows, :].astype(BF16)
        gl = _nn(sm, wg2_ref[...]) + bg_ref[...]
        g = -_softplus(-gl) * (1.0 / GLA_GATE_TAU)
        b = jnp.dot(tri_f, g, preferred_element_type=F32, precision=lax.Precision.HIGHEST)
        b_last = b[c - 1:c, :]
        b_mid = b[c // 2 - 1:c // 2, :]
        e_b = jnp.exp(b)
        e_qm = jnp.exp(b - b_mid)
        e_km = jnp.exp(b_mid - b)
        e_kd = jnp.exp(b_last - b)
        e_last = jnp.exp(b_last)
        q = q_ref[0, rows, :].astype(F32) * scale
        k = k_ref[0, rows, :].astype(F32)
        for h in range(GLA_HEADS):
            ks = slice(h * GLA_DK, (h + 1) * GLA_DK)
            vs = slice(h * GLA_DV, (h + 1) * GLA_DV)
            q_inter = (q[:, ks] * e_b[:, ks]).astype(BF16)
            q_intra = (q[:, ks] * e_qm[:, ks]).astype(BF16)
            kt = (k[:, ks] * e_km[:, ks]).astype(BF16)
            kd = (k[:, ks] * e_kd[:, ks]).astype(BF16)
            v = v_ref[0, rows, vs]
            st = st_scr[h]
            attn = jnp.where(tril, _nt(q_intra, kt), 0.0).astype(BF16)
            o = _nt(q_inter, st.astype(BF16)) + _nn(attn, v)
            st_scr[h] = st * e_last[:, ks] + _tn(v, kd)
            o = _rms(o, ng_ref[:, vs])
            o_ref[0, rows, vs] = (o * _silu(r_ref[0, rows, vs].astype(F32))).astype(BF16)
        return carry

    lax.fori_loop(0, n_chunks, chunk, 0, unroll=2)


def _gla(p, sm, wg2, bg, ng, *, cs=512):
    bsz, s, _ = p.shape
    cs = min(cs, s)
    return pl.pallas_call(
        functools.partial(_gla_body, n_chunks=cs // GLA_CHUNK),
        grid=(bsz, s // cs),
        in_specs=[
            pl.BlockSpec((1, cs, GLA_QK), lambda b, i: (b, i, 0)),
            pl.BlockSpec((1, cs, GLA_QK), lambda b, i: (b, i, 1)),
            pl.BlockSpec((1, cs, GLA_V), lambda b, i: (b, i, 1)),
            pl.BlockSpec((1, cs, GLA_V), lambda b, i: (b, i, 2)),
            pl.BlockSpec((1, cs, HY_SMALL), lambda b, i: (b, i, 0)),
            _resident(wg2.shape), _resident(bg.shape), _resident(ng.shape),
        ],
        out_specs=pl.BlockSpec((1, cs, GLA_V), lambda b, i: (b, i, 0)),
        out_shape=jax.ShapeDtypeStruct((bsz, s, GLA_V), BF16),
        scratch_shapes=[pltpu.VMEM((GLA_HEADS, GLA_DV, GLA_DK), F32)],
        compiler_params=_params("parallel", "arbitrary"),
        name="gla",
    )(p, p, p, p, sm, wg2, bg, ng)


def _ssd_body(z_ref, xs_ref, bc_ref, sm_ref, cwx_ref, cbx_ref, cwb_ref, cbb_ref, dtb_ref, a_ref, d_ref,
              ng_ref, ex_ref, exb_ref, o_ref, xpad, bpad, st_scr, y_scr, *, cs, n_chunks):
    first = pl.program_id(1) == 0
    halo = SUBLANES

    @pl.when(first)
    def _():
        st_scr[...] = jnp.zeros_like(st_scr)
        xpad[0:halo, :] = jnp.zeros((halo, SSD_INNER), F32)
        bpad[0:halo, :] = jnp.zeros((halo, 2 * SSD_BC), F32)

    @pl.when(jnp.logical_not(first))
    def _():
        xpad[0:halo, :] = xpad[cs:cs + halo, :]
        bpad[0:halo, :] = bpad[cs:cs + halo, :]

    xpad[halo:halo + cs, :] = xs_ref[0].astype(F32)
    bpad[halo:halo + cs, :] = bc_ref[0].astype(F32)

    c = SSD_CHUNK
    row = lax.broadcasted_iota(I32, (c, c), 0)
    col = lax.broadcasted_iota(I32, (c, c), 1)
    tril = row >= col
    tri_f = tril.astype(F32)
    lo_half = lax.broadcasted_iota(I32, (c, 2 * SSD_HEADDIM), 1) < SSD_HEADDIM
    gw = SSD_INNER // SSD_GROUPS

    def conv(pad_ref, w_ref, b_ref, start):
        win = pad_ref[pl.ds(start, c + halo), :]
        acc = b_ref[...]
        for kk in range(SSD_CONV):
            off = halo - (SSD_CONV - 1) + kk
            acc = acc + w_ref[kk:kk + 1, :] * win[off:off + c, :]
        return _silu(acc)

    def chunk(ci, carry):
        start = pl.multiple_of(ci * c, c)
        rows = pl.ds(start, c)
        xc = conv(xpad, cwx_ref, cbx_ref, start)
        bcc = conv(bpad, cwb_ref, cbb_ref, start)
        dt = _softplus(sm_ref[0, rows, :] + dtb_ref[...])
        da = dt * a_ref[...]
        acum = jnp.dot(tri_f, da, preferred_element_type=F32, precision=lax.Precision.HIGHEST)
        acum_t = acum.T
        a_last = acum[c - 1:c, :]
        fac = jnp.concatenate([dt, jnp.exp(acum), jnp.exp(a_last - acum)], axis=0).astype(BF16)
        fac_x = _nn(fac, exb_ref[...])
        dt_x, ea_x, w_x = fac_x[0:c], fac_x[c:2 * c], fac_x[2 * c:3 * c]
        dec_x = jnp.dot(jnp.broadcast_to(jnp.exp(a_last), (SUBLANES, HY_SMALL)), ex_ref[...],
                        preferred_element_type=F32, precision=lax.Precision.HIGHEST)[0:1]
        xdt = xc * dt_x
        xdt_b = xdt.astype(BF16)
        xdtw_b = (xdt * w_x).astype(BF16)
        for grp in range(SSD_GROUPS):
            gs = slice(grp * gw, (grp + 1) * gw)
            bm = bcc[:, grp * SSD_STATE:(grp + 1) * SSD_STATE].astype(BF16)
            cm = bcc[:, SSD_BC + grp * SSD_STATE:SSD_BC + (grp + 1) * SSD_STATE].astype(BF16)
            scores = _nt(cm, bm)
            st = st_scr[:, gs]
            y_inter = _nn(cm, st.astype(BF16)) * ea_x[:, gs]
            st_scr[:, gs] = st * dec_x[:, gs] + _tn(bm, xdtw_b[:, gs])
            for pair in range(SSD_HEADS_PER_GROUP // 2):
                h0 = grp * SSD_HEADS_PER_GROUP + 2 * pair
                ps = slice(h0 * SSD_HEADDIM, (h0 + 2) * SSD_HEADDIM)
                lhs = []
                for h in (h0, h0 + 1):
                    ln = HY_DT_LANE + h
                    seg = jnp.where(tril, jnp.exp(acum[:, ln:ln + 1] - acum_t[ln:ln + 1, :]), 0.0)
                    lhs.append((scores * seg).astype(BF16))
                slab = xdt_b[:, ps]
                zero = jnp.zeros_like(slab)
                rhs = jnp.concatenate([jnp.where(lo_half, slab, zero), jnp.where(lo_half, zero, slab)], axis=0)
                y_pair = _nn(jnp.concatenate(lhs, axis=1), rhs)
                y_scr[:, ps] = (y_pair + y_inter[:, 2 * pair * SSD_HEADDIM:(2 * pair + 2) * SSD_HEADDIM]
                                + d_ref[:, ps] * xc[:, ps])
        y = y_scr[...] * _silu(z_ref[0, rows, :].astype(F32))
        for grp in range(SSD_GROUPS):
            gs = slice(grp * gw, (grp + 1) * gw)
            o_ref[0, rows, gs] = _rms(y[:, gs], ng_ref[:, gs]).astype(BF16)
        return carry

    lax.fori_loop(0, n_chunks, chunk, 0)


def _head_expander():
    e = np.zeros((HY_SMALL, SSD_INNER), np.float32)
    for h in range(SSD_HEADS):
        e[HY_DT_LANE + h, h * SSD_HEADDIM:(h + 1) * SSD_HEADDIM] = 1.0
    return jnp.asarray(e)


def _ssd(p, sm, cwx, cbx, cwb, cbb, dtb, a_pad, d_full, ng, *, cs=512):
    bsz, s, _ = p.shape
    cs = min(cs, s)
    ex = _head_expander()
    consts = (cwx, cbx, cwb, cbb, dtb, a_pad, d_full, ng, ex, ex.astype(BF16))
    return pl.pallas_call(
        functools.partial(_ssd_body, cs=cs, n_chunks=cs // SSD_CHUNK),
        grid=(bsz, s // cs),
        in_specs=[
            pl.BlockSpec((1, cs, SSD_INNER), lambda b, i: (b, i, 3)),
            pl.BlockSpec((1, cs, SSD_INNER), lambda b, i: (b, i, 4)),
            pl.BlockSpec((1, cs, 2 * SSD_BC), lambda b, i: (b, i, 10)),
            pl.BlockSpec((1, cs, HY_SMALL), lambda b, i: (b, i, 0)),
        ] + [_resident(a.shape) for a in consts],
        out_specs=pl.BlockSpec((1, cs, SSD_INNER), lambda b, i: (b, i, 0)),
        out_shape=jax.ShapeDtypeStruct((bsz, s, SSD_INNER), BF16),
        scratch_shapes=[
            pltpu.VMEM((cs + SUBLANES, SSD_INNER), F32),
            pltpu.VMEM((cs + SUBLANES, 2 * SSD_BC), F32),
            pltpu.VMEM((SSD_STATE, SSD_INNER), F32),
            pltpu.VMEM((SSD_CHUNK, SSD_INNER), F32),
        ],
        compiler_params=_params("parallel", "arbitrary"),
        name="ssd",
    )(p, p, p, sm, *consts)


def _out_proj_body(x_ref, a_ref, b_ref, wa_ref, wb_ref, o_ref):
    o_ref[...] = x_ref[...] + _nn(a_ref[...], wa_ref[...]) + _nn(b_ref[...], wb_ref[...])


def _out_proj(x, a, b, wa, wb, *, tm=512):
    t, d = x.shape
    tm = min(tm, t)
    tok = lambda w: pl.BlockSpec((tm, w), lambda i: (i, 0))
    return pl.pallas_call(
        _out_proj_body,
        grid=(t // tm,),
        in_specs=[tok(d), tok(a.shape[1]), tok(b.shape[1]),
                  _resident(wa.shape), _resident(wb.shape)],
        out_specs=tok(d),
        out_shape=jax.ShapeDtypeStruct((t, d), F32),
        compiler_params=_params("parallel"),
        name="hy_out_proj",
    )(x, a, b, wa, wb)


DSA_H_WIDTH = 512
DSA_LAT_EXT = DSA_LATENT + 16


def _dsa_proj_body(x_ref, g_ref, wh_ref, wq_ref, qg_ref, kvg_ref, lng_ref, lnb_ref,
                   q_ref, qi_ref, kv_ref, kvt_ref, ki_ref, wt_ref):
    h = _rms(x_ref[...], g_ref[...]).astype(BF16)
    c = _nn(h, wh_ref[...])
    q_lat = _rms(c[:, :DSA_Q_RANK], qg_ref[...]).astype(BF16)
    qq = _nn(q_lat, wq_ref[...])
    nq = DSA_HEADS * DSA_LATENT
    q_ref[...] = (qq[:, :nq] * (DSA_LATENT ** -0.5 * LOG2E)).astype(BF16)
    qi_ref[...] = qq[:, nq:].astype(BF16)
    kv = _rms(c[:, DSA_Q_RANK:DSA_Q_RANK + DSA_LATENT], kvg_ref[...])
    kv_ref[...] = kv.astype(BF16)
    kvt_ref[0:DSA_LATENT, :] = kv.T.astype(BF16)
    ones_row = lax.broadcasted_iota(I32, (DSA_LAT_EXT - DSA_LATENT, kv.shape[0]), 0) == 0
    kvt_ref[DSA_LATENT:, :] = ones_row.astype(F32).astype(BF16)
    tail = c[:, DSA_Q_RANK + DSA_LATENT:]
    kr = tail[:, :IDX_DIM]
    mu = jnp.mean(kr, axis=-1, keepdims=True)
    var = jnp.mean(jnp.square(kr - mu), axis=-1, keepdims=True)
    ki = (kr - mu) * lax.rsqrt(var + EPS) * lng_ref[...] + lnb_ref[...]
    ki_ref[...] = ki.astype(BF16)
    tail_t = (tail * (IDX_HEADS ** -0.5 * IDX_DIM ** -0.5)).T
    wt_ref[...] = tail_t[IDX_DIM:IDX_DIM + IDX_HEADS, :]


def _dsa_proj(x, g, wh, wq, qg, kvg, lng, lnb, *, tm=512):
    t, d = x.shape
    tm = min(tm, t)
    nq = DSA_HEADS * DSA_LATENT
    ni = IDX_HEADS * IDX_DIM
    full = lambda a: _resident(a.shape)
    tok = lambda w: pl.BlockSpec((tm, w), lambda i: (i, 0))
    return pl.pallas_call(
        _dsa_proj_body,
        grid=(t // tm,),
        in_specs=[tok(d), full(g), full(wh), full(wq), full(qg), full(kvg), full(lng), full(lnb)],
        out_specs=[tok(nq), tok(ni), tok(DSA_LATENT),
                   pl.BlockSpec((DSA_LAT_EXT, tm), lambda i: (0, i)),
                   tok(IDX_DIM),
                   pl.BlockSpec((IDX_HEADS, tm), lambda i: (0, i))],
        out_shape=[
            jax.ShapeDtypeStruct((t, nq), BF16),
            jax.ShapeDtypeStruct((t, ni), BF16),
            jax.ShapeDtypeStruct((t, DSA_LATENT), BF16),
            jax.ShapeDtypeStruct((DSA_LAT_EXT, t), BF16),
            jax.ShapeDtypeStruct((t, IDX_DIM), BF16),
            jax.ShapeDtypeStruct((IDX_HEADS, t), F32),
        ],
        compiler_params=_params("parallel"),
        name="dsa_proj",
    )(x, g, wh, wq, qg, kvg, lng, lnb)


DSA_TQ = 256
PART_ROWS = 4 * SUBLANES


def _dsa_attn_body(q_ref, qi_ref, wt_ref, ki_ref, kv_ref, kvt_ref, bias_ref, x_ref, wuv_ref, wo_ref,
                   o_ref, key_scr, top_scr, s_scr, acc_scr, m_scr, bm_scr, al_scr, mb_scr, u_scr, *, topk):
    tq = DSA_TQ
    tk = DSA_TQ
    qb = pl.program_id(1)
    krow = lax.broadcasted_iota(I32, (tk, tq), 0)
    qcol = lax.broadcasted_iota(I32, (tk, tq), 1)
    causal = krow <= qcol

    def score_block(row0, n_rows, diag):
        rows = pl.ds(pl.multiple_of(row0, tk), n_rows)
        kk = ki_ref[rows, :]
        s = jnp.zeros((n_rows, tq), F32)
        for h in range(IDX_HEADS):
            z = _nt(kk, qi_ref[:, h * IDX_DIM:(h + 1) * IDX_DIM])
            s = s + jnp.maximum(z, 0.0) * wt_ref[h:h + 1, :]
        if diag:
            s = jnp.where(causal, s, -jnp.inf)
        bits = pltpu.bitcast(s, I32)
        key_scr[rows, :] = bits ^ ((bits >> 31) & 0x7FFFFFFF)
        top_scr[rows, :] = pltpu.bitcast(bits & -65536, F32).astype(BF16)

    def score_loop(kp, carry):
        score_block(kp * (2 * tk), 2 * tk, False)
        return carry

    lax.fori_loop(0, qb // 2, score_loop, 0)

    @pl.when(qb % 2 == 1)
    def _():
        score_block((qb - 1) * tk, tk, False)

    score_block(qb * tk, tk, True)
    pad_rows = pl.ds(pl.multiple_of((qb + 1) * tk, tk), tk)
    key_scr[pad_rows, :] = jnp.full((tk, tq), INT_MIN, I32)
    top_scr[pad_rows, :] = jnp.full((tk, tq), -jnp.inf, BF16)
    n_pairs = (qb + 2) // 2

    pack_rows = 4 * BF16_SUBLANES

    def top_step(it, upper):
        cand = upper | (jnp.int32(1) << (15 - it))
        bits = jnp.where(cand >= 0x8000, cand ^ 0x8000, (~cand) & 0xFFFF)
        cand_f = pltpu.bitcast(bits << 16, F32).astype(BF16)

        def count(kp, cnt):
            rows = pl.ds(pl.multiple_of(kp * (2 * tk), 2 * tk), 2 * tk)
            one = jnp.where(top_scr[rows, :] >= cand_f, jnp.ones((), BF16), jnp.zeros((), BF16))
            for i in range(2 * tk // pack_rows):
                cnt = cnt + one[i * pack_rows:(i + 1) * pack_rows]
            return cnt

        cnt = lax.fori_loop(0, n_pairs, count, jnp.zeros((pack_rows, tq), BF16))
        tot = jnp.sum(cnt.astype(F32), axis=0, keepdims=True)
        return jnp.where(tot >= topk, cand, upper)

    upper = lax.fori_loop(0, 16, top_step, jnp.zeros((1, tq), I32))

    def bit_step(it, thr):
        cand = thr ^ (jnp.int32(1) << (31 - it))

        def count(kp, cnt):
            rows = pl.ds(pl.multiple_of(kp * (2 * tk), 2 * tk), 2 * tk)
            ge = jnp.where(key_scr[rows, :] >= cand, 1, 0)
            return cnt + jnp.sum(ge.reshape(2 * tk // PART_ROWS, PART_ROWS, tq), axis=0)

        cnt = lax.fori_loop(0, n_pairs, count, jnp.zeros((PART_ROWS, tq), I32))
        tot = jnp.sum(cnt, axis=0, keepdims=True)
        return jnp.where(tot >= topk, cand, thr)

    thr = lax.fori_loop(16, 32, bit_step, (upper << 16) ^ INT_MIN)

    m_scr[...] = jnp.full(m_scr.shape, NEG_BIG, F32)
    acc_scr[...] = jnp.zeros_like(acc_scr)
    key_minus_query = krow - qcol

    def block_rows(kb):
        return pl.ds(pl.multiple_of(kb * tk, tk), tk)

    def tiles(x):
        return x.reshape(x.shape[0] // SUBLANES, SUBLANES, tq)

    def block_max(s):
        part = jnp.max(s.reshape(tk // PART_ROWS, PART_ROWS, tq), axis=0)
        m8 = jnp.max(tiles(part), axis=0)
        for shift in (4, 2, 1):
            m8 = jnp.maximum(m8, pltpu.roll(m8, shift, axis=0))
        return m8

    def attn_step(kb, cur, with_logits, with_probs):
        prev = 1 - cur
        if with_logits:
            kvb = kv_ref[block_rows(kb), :]
            sel = jnp.logical_and(key_scr[block_rows(kb), :] >= thr, key_minus_query <= (qb - kb) * tk)
            cap = jnp.where(sel, jnp.inf, NEG_BIG)
        if with_probs:
            kvtb = kvt_ref[:, block_rows(kb - 1)]
        for h in range(DSA_HEADS):
            if with_logits:
                s = _nt(kvb, q_ref[:, h * DSA_LATENT:(h + 1) * DSA_LATENT])
                s = jnp.minimum(s, cap)
                s_scr[cur, h] = tiles(s)
                bm_scr[h] = block_max(s)
            if with_probs:
                p = jnp.exp2(s_scr[prev, h] - mb_scr[prev, h][None]).reshape(tk, tq)
                acc_scr[h] = acc_scr[h] * al_scr[prev, h][None] + tiles(_nn(kvtb, p.astype(BF16)))
        if with_logits:
            @pl.when(kb >= qb - 1)
            def _():
                for h in range(DSA_HEADS):
                    s = s_scr[cur, h].reshape(tk, tq) + bias_ref[h, qb - kb]
                    s_scr[cur, h] = tiles(s)
                    bm_scr[h] = block_max(s)

            m_old = m_scr[...]
            m_new = jnp.maximum(m_old, bm_scr[...])
            al_scr[cur] = jnp.exp2(m_old - m_new)
            mb_scr[cur] = m_new
            m_scr[...] = m_new

    attn_step(0, 0, True, False)

    def attn_pair(j, carry):
        attn_step(2 * j + 1, 1, True, True)
        attn_step(2 * j + 2, 0, True, True)
        return carry

    lax.fori_loop(0, qb // 2, attn_pair, 0)

    @pl.when(qb % 2 == 1)
    def _():
        attn_step(qb, 1, True, True)
        attn_step(qb + 1, 0, False, True)

    @pl.when(qb % 2 == 0)
    def _():
        attn_step(qb + 1, 1, False, True)

    for h in range(DSA_HEADS):
        acc = acc_scr[h].reshape(DSA_LAT_EXT, tq)
        o_t = (acc[0:DSA_LATENT] / acc[DSA_LATENT:DSA_LATENT + 1]).astype(BF16)
        u_scr[h * DSA_VDIM:(h + 1) * DSA_VDIM, :] = _nn(wuv_ref[h], o_t).astype(BF16)
    y_t = _nn(wo_ref[...], u_scr[...])
    o_ref[...] = x_ref[...] + y_t.T


def _dsa_attn(x, q, qi, wt, ki, kv, kvt, bias_t, wuv_t, wo_t, *, bsz, s):
    t, d = x.shape
    tq = DSA_TQ
    nq = s // tq
    topk = min(TOPK_MAX, s // 4)
    full = lambda a: _resident(a.shape)
    return pl.pallas_call(
        functools.partial(_dsa_attn_body, topk=topk),
        grid=(bsz, nq),
        in_specs=[
            pl.BlockSpec((tq, q.shape[1]), lambda b, i: (b * nq + i, 0)),
            pl.BlockSpec((tq, qi.shape[1]), lambda b, i: (b * nq + i, 0)),
            pl.BlockSpec((IDX_HEADS, tq), lambda b, i: (0, b * nq + i)),
            pl.BlockSpec((s, IDX_DIM), lambda b, i: (b, 0)),
            pl.BlockSpec((s, DSA_LATENT), lambda b, i: (b, 0)),
            pl.BlockSpec((DSA_LAT_EXT, s), lambda b, i: (0, b)),
            full(bias_t),
            pl.BlockSpec((tq, d), lambda b, i: (b * nq + i, 0)),
            full(wuv_t), full(wo_t),
        ],
        out_specs=pl.BlockSpec((tq, d), lambda b, i: (b * nq + i, 0)),
        out_shape=jax.ShapeDtypeStruct((t, d), F32),
        scratch_shapes=[
            pltpu.VMEM((s + tq, tq), I32),
            pltpu.VMEM((s + tq, tq), BF16),
            pltpu.VMEM((2, DSA_HEADS, tq // SUBLANES, SUBLANES, tq), F32),
            pltpu.VMEM((DSA_HEADS, DSA_LAT_EXT // SUBLANES, SUBLANES, tq), F32),
            pltpu.VMEM((DSA_HEADS, SUBLANES, tq), F32),
            pltpu.VMEM((DSA_HEADS, SUBLANES, tq), F32),
            pltpu.VMEM((2, DSA_HEADS, SUBLANES, tq), F32),
            pltpu.VMEM((2, DSA_HEADS, SUBLANES, tq), F32),
            pltpu.VMEM((DSA_HEADS * DSA_VDIM, tq), BF16),
        ],
        compiler_params=_params("parallel", "arbitrary"),
        name="dsa_attn",
    )(q, qi, wt, ki, kv, kvt, bias_t, x, wuv_t, wo_t)


def _t5_bucket(rel):
    max_exact = REL_BUCKETS // 2
    relf = jnp.maximum(rel, 1).astype(F32)
    large = max_exact + (jnp.log(relf / max_exact) / math.log(REL_MAX_DIST / max_exact)
                         * (REL_BUCKETS - max_exact)).astype(I32)
    large = jnp.minimum(large, REL_BUCKETS - 1)
    return jnp.where(rel < max_exact, rel, large)


def _bias_tiles(rel_bias):
    tq = DSA_TQ
    kk = jnp.arange(tq)[:, None]
    qq = jnp.arange(tq)[None, :]
    rel = jnp.stack([qq - kk, qq - kk + tq])
    table = (rel_bias - rel_bias[REL_BUCKETS - 1][None, :]) * LOG2E
    onehot = (_t5_bucket(jnp.maximum(rel, 0))[..., None] == jnp.arange(REL_BUCKETS)).astype(F32)
    return jnp.einsum("kabn,nh->hkab", onehot, table, precision=lax.Precision.HIGHEST)


def _hy_in_weight(w_in):
    sizes = (GLA_QK, GLA_QK, GLA_V, GLA_V, GLA_GATE_RANK, SSD_INNER, SSD_INNER + 2 * SSD_BC, SSD_HEADS)
    offs = np.cumsum((0,) + sizes)
    q, k, v, r, g_lr, z, xbc, dt = [w_in[:, offs[i]:offs[i + 1]] for i in range(len(sizes))]
    pad = jnp.zeros((w_in.shape[0], HY_SMALL - GLA_GATE_RANK - SSD_HEADS), w_in.dtype)
    return jnp.concatenate([q, k, v, r, z, xbc, g_lr, dt, pad], axis=1).astype(BF16)


def kernel(x, ffn_norm_g, ffn_w_gate, ffn_w_up, ffn_w_down, mix_norm_g, hy_w_in, gla_w_gate2, gla_b_gate, gla_norm_g, ssd_conv_w, ssd_conv_b, ssd_dt_bias, ssd_a_log, ssd_d, ssd_norm_g, hy_w_out, dsa_w_dq, dsa_q_norm_g, dsa_w_uq, dsa_w_dkv, dsa_kv_norm_g, dsa_w_uv, dsa_w_o, idx_w_q, idx_w_k, idx_ln_g, idx_ln_b, idx_w_w, rel_bias, final_norm_g):
    bsz, s, d = x.shape
    t = bsz * s
    xt = x.reshape(t, d)
    row = lambda a: a.reshape(1, -1).astype(F32)

    wg_all, wu_all, wd_all = (w.astype(BF16) for w in (ffn_w_gate, ffn_w_up, ffn_w_down))

    def ffn(xt, layer, half, final):
        return _ffn(xt, row(ffn_norm_g[layer, half]), wg_all, wu_all, wd_all, row(final_norm_g),
                    layer=layer, half=half, final_norm=final)

    xt = ffn(xt, 0, 0, False)
    p, sm = _in_proj(xt, row(mix_norm_g[0]), _hy_in_weight(hy_w_in[0]))
    p = p.reshape(bsz, s, HY_MAIN)
    sm = sm.reshape(bsz, s, HY_SMALL)
    wg2 = jnp.zeros((HY_SMALL, GLA_QK), F32).at[:GLA_GATE_RANK].set(gla_w_gate2[0]).astype(BF16)
    o_gla = _gla(p, sm, wg2, row(gla_b_gate[0]), row(gla_norm_g[0]))
    cw, cb = ssd_conv_w[0].astype(F32), ssd_conv_b[0].astype(F32)
    lane_pad = lambda v: jnp.zeros((1, HY_SMALL), F32).at[0, HY_DT_LANE:HY_DT_LANE + SSD_HEADS].set(v)
    y_ssd = _ssd(p, sm, cw[:, :SSD_INNER], row(cb[:SSD_INNER]), cw[:, SSD_INNER:], row(cb[SSD_INNER:]),
                 lane_pad(ssd_dt_bias[0].astype(F32)), lane_pad(-jnp.exp(ssd_a_log[0].astype(F32))),
                 row(jnp.repeat(ssd_d[0], SSD_HEADDIM)), row(ssd_norm_g[0]))
    w_out = hy_w_out[0].astype(BF16)
    xt = _out_proj(xt, o_gla.reshape(t, GLA_V), y_ssd.reshape(t, SSD_INNER), w_out[:GLA_V], w_out[GLA_V:])
    xt = ffn(xt, 0, 1, False)

    xt = ffn(xt, 1, 0, False)
    wh = jnp.concatenate(
        [dsa_w_dq[0], dsa_w_dkv[0], idx_w_k[0], idx_w_w[0],
         jnp.zeros((d, DSA_H_WIDTH - DSA_Q_RANK - DSA_LATENT - IDX_DIM - IDX_HEADS), F32)], axis=1).astype(BF16)
    wq = jnp.concatenate([dsa_w_uq[0], idx_w_q[0]], axis=1).astype(BF16)
    q, qi, kv, kvt, ki, wt = _dsa_proj(xt, row(mix_norm_g[1]), wh, wq, row(dsa_q_norm_g[0]),
                                       row(dsa_kv_norm_g[0]), row(idx_ln_g[0]), row(idx_ln_b[0]))
    wuv_t = jnp.transpose(dsa_w_uv[0], (0, 2, 1)).astype(BF16)
    wo_t = dsa_w_o[0].T.astype(BF16)
    xt = _dsa_attn(xt, q, qi, wt, ki, kv, kvt, _bias_tiles(rel_bias.astype(F32)), wuv_t, wo_t, bsz=bsz, s=s)
    xt = ffn(xt, 1, 1, True)
    return xt.reshape(bsz, s, d)
```

```python
import functools
import math

import numpy as np
import jax
import jax.numpy as jnp
from jax import lax
from jax.experimental import pallas as pl
from jax.experimental.pallas import tpu as pltpu

F32 = jnp.float32
BF16 = jnp.bfloat16
I32 = jnp.int32

EPS = 1e-6
D_MODEL = 1024
D_FF = 2816
GLA_HEADS = 4
GLA_DK = 128
GLA_DV = 256
GLA_GATE_RANK = 16
GLA_GATE_TAU = 16.0
GLA_CHUNK = 128
GLA_QK = GLA_HEADS * GLA_DK
GLA_V = GLA_HEADS * GLA_DV
SSD_HEADS = 16
SSD_HEADDIM = 64
SSD_STATE = 128
SSD_GROUPS = 2
SSD_CONV = 4
SSD_CHUNK = 128
SSD_INNER = SSD_HEADS * SSD_HEADDIM
SSD_BC = SSD_GROUPS * SSD_STATE
SSD_HEADS_PER_GROUP = SSD_HEADS // SSD_GROUPS
DSA_HEADS = 16
DSA_Q_RANK = 256
DSA_LATENT = 128
DSA_VDIM = 64
IDX_HEADS = 16
IDX_DIM = 64
TOPK_MAX = 256
REL_BUCKETS = 32
REL_MAX_DIST = 128

LANES = 128
SUBLANES = 8
BF16_SUBLANES = 16
VMEM_LIMIT = 56 * 1024 * 1024

HY_SMALL = LANES
HY_MAIN = 2 * GLA_QK + 2 * GLA_V + SSD_INNER + SSD_INNER + 2 * SSD_BC
HY_WIDTH = HY_MAIN + HY_SMALL
HY_DT_LANE = GLA_GATE_RANK

NEG_BIG = -1e30
LOG2E = math.log2(math.e)
INT_MIN = -2 ** 31


def _nt(a, b):
    return lax.dot_general(a, b, (((1,), (1,)), ((), ())), preferred_element_type=F32)


def _tn(a, b):
    return lax.dot_general(a, b, (((0,), (0,)), ((), ())), preferred_element_type=F32)


def _nn(a, b):
    return jnp.dot(a, b, preferred_element_type=F32)


def _silu(a):
    return a / (1.0 + jnp.exp(-a))


def _softplus(a):
    return jnp.maximum(a, 0.0) + jnp.log1p(jnp.exp(-jnp.abs(a)))


def _rms(x, g):
    return x * lax.rsqrt(jnp.mean(x * x, axis=-1, keepdims=True) + EPS) * g


def _params(*sem):
    return pltpu.CompilerParams(dimension_semantics=sem, vmem_limit_bytes=VMEM_LIMIT)


def _ffn_body(x_ref, g_ref, wg_ref, wu_ref, wd_ref, fg_ref, o_ref, *, final_norm):
    x = x_ref[...]
    h = _rms(x, g_ref[...]).astype(BF16)
    a = _nn(h, wg_ref[...])
    b = _nn(h, wu_ref[...])
    t = (_silu(a) * b).astype(BF16)
    y = x + 0.5 * _nn(t, wd_ref[...])
    if final_norm:
        y = _rms(y, fg_ref[...])
    o_ref[...] = y


def _resident(shape):
    return pl.BlockSpec(shape, lambda *_: (0,) * len(shape), pipeline_mode=pl.Buffered(1))


def _ffn(x, g, wg, wu, wd, fg, *, layer, half, final_norm, tm=512):
    t, d = x.shape
    tm = min(tm, t)
    pick = lambda w: pl.BlockSpec((None, None) + w.shape[2:], lambda i: (layer, half, 0, 0),
                                  pipeline_mode=pl.Buffered(1))
    return pl.pallas_call(
        functools.partial(_ffn_body, final_norm=final_norm),
        grid=(t // tm,),
        in_specs=[
            pl.BlockSpec((tm, d), lambda i: (i, 0)),
            _resident(g.shape), pick(wg), pick(wu), pick(wd),
            _resident(fg.shape),
        ],
        out_specs=pl.BlockSpec((tm, d), lambda i: (i, 0)),
        out_shape=jax.ShapeDtypeStruct((t, d), F32),
        compiler_params=_params("parallel"),
        name="ffn",
    )(x, g, wg, wu, wd, fg)


def _in_proj_body(x_ref, g_ref, w_ref, o_ref, sm_ref):
    h = _rms(x_ref[...], g_ref[...]).astype(BF16)
    o_ref[...] = _nn(h, w_ref[:, :HY_MAIN]).astype(BF16)
    sm_ref[...] = _nn(h, w_ref[:, HY_MAIN:])


def _in_proj(x, g, w, *, tm=512):
    t, d = x.shape
    tm = min(tm, t)
    return pl.pallas_call(
        _in_proj_body,
        grid=(t // tm,),
        in_specs=[pl.BlockSpec((tm, d), lambda i: (i, 0)), _resident(g.shape), _resident(w.shape)],
        out_specs=[pl.BlockSpec((tm, HY_MAIN), lambda i: (i, 0)), pl.BlockSpec((tm, HY_SMALL), lambda i: (i, 0))],
        out_shape=[jax.ShapeDtypeStruct((t, HY_MAIN), BF16), jax.ShapeDtypeStruct((t, HY_SMALL), F32)],
        compiler_params=_params("parallel"),
        name="hy_in_proj",
    )(x, g, w)


def _gla_body(q_ref, k_ref, v_ref, r_ref, sm_ref, wg2_ref, bg_ref, ng_ref, o_ref, st_scr, *, n_chunks):
    @pl.when(pl.program_id(1) == 0)
    def _():
        st_scr[...] = jnp.zeros_like(st_scr)

    c = GLA_CHUNK
    row = lax.broadcasted_iota(I32, (c, c), 0)
    col = lax.broadcasted_iota(I32, (c, c), 1)
    tril = row >= col
    tri_f = tril.astype(F32)
    scale = GLA_DK ** -0.5

    def chunk(ci, carry):
        rows = pl.ds(pl.multiple_of(ci * c, c), c)
        sm = sm_ref[0, rows, :].astype(BF16)
        gl = _nn(sm, wg2_ref[...]) + bg_ref[...]
        g = -_softplus(-gl) * (1.0 / GLA_GATE_TAU)
        b = jnp.dot(tri_f, g, preferred_element_type=F32, precision=lax.Precision.HIGHEST)
        b_last = b[c - 1:c, :]
        b_mid = b[c // 2 - 1:c // 2, :]
        e_b = jnp.exp(b)
        e_qm = jnp.exp(b - b_mid)
        e_km = jnp.exp(b_mid - b)
        e_kd = jnp.exp(b_last - b)
        e_last = jnp.exp(b_last)
        q = q_ref[0, rows, :].astype(F32) * scale
        k = k_ref[0, rows, :].astype(F32)
        for h in range(GLA_HEADS):
            ks = slice(h * GLA_DK, (h + 1) * GLA_DK)
            vs = slice(h * GLA_DV, (h + 1) * GLA_DV)
            q_inter = (q[:, ks] * e_b[:, ks]).astype(BF16)
            q_intra = (q[:, ks] * e_qm[:, ks]).astype(BF16)
            kt = (k[:, ks] * e_km[:, ks]).astype(BF16)
            kd = (k[:, ks] * e_kd[:, ks]).astype(BF16)
            v = v_ref[0, rows, vs]
            st = st_scr[h]
            attn = jnp.where(tril, _nt(q_intra, kt), 0.0).astype(BF16)
            o = _nt(q_inter, st.astype(BF16)) + _nn(attn, v)
            st_scr[h] = st * e_last[:, ks] + _tn(v, kd)
            o = _rms(o, ng_ref[:, vs])
            o_ref[0, rows, vs] = (o * _silu(r_ref[0, rows, vs].astype(F32))).astype(BF16)
        return carry

    lax.fori_loop(0, n_chunks, chunk, 0, unroll=2)


def _gla(p, sm, wg2, bg, ng, *, cs=512):
    bsz, s, _ = p.shape
    cs = min(cs, s)
    return pl.pallas_call(
        functools.partial(_gla_body, n_chunks=cs // GLA_CHUNK),
        grid=(bsz, s // cs),
        in_specs=[
            pl.BlockSpec((1, cs, GLA_QK), lambda b, i: (b, i, 0)),
            pl.BlockSpec((1, cs, GLA_QK), lambda b, i: (b, i, 1)),
            pl.BlockSpec((1, cs, GLA_V), lambda b, i: (b, i, 1)),
            pl.BlockSpec((1, cs, GLA_V), lambda b, i: (b, i, 2)),
            pl.BlockSpec((1, cs, HY_SMALL), lambda b, i: (b, i, 0)),
            _resident(wg2.shape), _resident(bg.shape), _resident(ng.shape),
        ],
        out_specs=pl.BlockSpec((1, cs, GLA_V), lambda b, i: (b, i, 0)),
        out_shape=jax.ShapeDtypeStruct((bsz, s, GLA_V), BF16),
        scratch_shapes=[pltpu.VMEM((GLA_HEADS, GLA_DV, GLA_DK), F32)],
        compiler_params=_params("parallel", "arbitrary"),
        name="gla",
    )(p, p, p, p, sm, wg2, bg, ng)


def _ssd_body(z_ref, xs_ref, bc_ref, sm_ref, cwx_ref, cbx_ref, cwb_ref, cbb_ref, dtb_ref, a_ref, d_ref,
              ng_ref, exb_ref, shift_ref, o_ref, xpad, bpad, st_scr, y_scr, *, cs, n_chunks):
    first = pl.program_id(1) == 0
    c = SSD_CHUNK
    halo = c

    @pl.when(first)
    def _():
        st_scr[...] = jnp.zeros_like(st_scr)
        xpad[0:halo, :] = jnp.zeros((halo, SSD_INNER), BF16)
        bpad[0:halo, :] = jnp.zeros((halo, 2 * SSD_BC), BF16)

    @pl.when(jnp.logical_not(first))
    def _():
        xpad[0:halo, :] = xpad[cs:cs + halo, :]
        bpad[0:halo, :] = bpad[cs:cs + halo, :]

    xpad[halo:halo + cs, :] = xs_ref[0]
    bpad[halo:halo + cs, :] = bc_ref[0]

    row = lax.broadcasted_iota(I32, (c, c), 0)
    col = lax.broadcasted_iota(I32, (c, c), 1)
    tril = row >= col
    tri_f = tril.astype(F32)
    lo_half = lax.broadcasted_iota(I32, (c, 2 * SSD_HEADDIM), 1) < SSD_HEADDIM
    gw = SSD_INNER // SSD_GROUPS

    def conv(pad_ref, w_ref, b_ref, start):
        win = pad_ref[pl.ds(start, 2 * c), :]
        delayed = _nn(shift_ref[...], win)
        acc = b_ref[...] + w_ref[SSD_CONV - 1:SSD_CONV, :] * win[c:2 * c, :].astype(F32)
        for kk in range(SSD_CONV - 1):
            acc = acc + w_ref[kk:kk + 1, :] * delayed[kk * c:(kk + 1) * c, :]
        return _silu(acc)

    def spread_exact(v):
        hi = v.astype(BF16)
        rest = v - hi.astype(F32)
        mid = rest.astype(BF16)
        lo = (rest - mid.astype(F32)).astype(BF16)
        pieces = jnp.concatenate([jnp.broadcast_to(p, (BF16_SUBLANES, HY_SMALL)) for p in (hi, mid, lo)], axis=0)
        out = _nn(pieces, exb_ref[...])
        return out[0:1] + out[BF16_SUBLANES:BF16_SUBLANES + 1] + out[2 * BF16_SUBLANES:2 * BF16_SUBLANES + 1]

    def chunk(ci, carry):
        start = pl.multiple_of(ci * c, c)
        rows = pl.ds(start, c)
        xc = conv(xpad, cwx_ref, cbx_ref, start)
        bcc = conv(bpad, cwb_ref, cbb_ref, start)
        dt = _softplus(sm_ref[0, rows, :] + dtb_ref[...])
        da = dt * a_ref[...]
        acum = jnp.dot(tri_f, da, preferred_element_type=F32, precision=lax.Precision.HIGHEST)
        acum_t = acum.T
        a_last = acum[c - 1:c, :]
        fac = jnp.concatenate([dt, jnp.exp(acum), jnp.exp(a_last - acum)], axis=0).astype(BF16)
        fac_x = _nn(fac, exb_ref[...])
        dt_x, ea_x, w_x = fac_x[0:c], fac_x[c:2 * c], fac_x[2 * c:3 * c]
        dec_x = spread_exact(jnp.exp(a_last))
        xdt = xc * dt_x
        xdt_b = xdt.astype(BF16)
        xdtw_b = (xdt * w_x).astype(BF16)
        for grp in range(SSD_GROUPS):
            gs = slice(grp * gw, (grp + 1) * gw)
            bm = bcc[:, grp * SSD_STATE:(grp + 1) * SSD_STATE].astype(BF16)
            cm = bcc[:, SSD_BC + grp * SSD_STATE:SSD_BC + (grp + 1) * SSD_STATE].astype(BF16)
            scores = _nt(cm, bm)
            st = st_scr[:, gs]
            y_inter = _nn(cm, st.astype(BF16)) * ea_x[:, gs]
            st_scr[:, gs] = st * dec_x[:, gs] + _tn(bm, xdtw_b[:, gs])
            for pair in range(SSD_HEADS_PER_GROUP // 2):
                h0 = grp * SSD_HEADS_PER_GROUP + 2 * pair
                ps = slice(h0 * SSD_HEADDIM, (h0 + 2) * SSD_HEADDIM)
                lhs = []
                for h in (h0, h0 + 1):
                    ln = HY_DT_LANE + h
                    seg = jnp.where(tril, jnp.exp(acum[:, ln:ln + 1] - acum_t[ln:ln + 1, :]), 0.0)
                    lhs.append((scores * seg).astype(BF16))
                slab = xdt_b[:, ps]
                zero = jnp.zeros_like(slab)
                rhs = jnp.concatenate([jnp.where(lo_half, slab, zero), jnp.where(lo_half, zero, slab)], axis=0)
                y_pair = _nn(jnp.concatenate(lhs, axis=1), rhs)
                y_scr[:, ps] = (y_pair + y_inter[:, 2 * pair * SSD_HEADDIM:(2 * pair + 2) * SSD_HEADDIM]
                                + d_ref[:, ps] * xc[:, ps])
        y = y_scr[...] * _silu(z_ref[0, rows, :].astype(F32))
        for grp in range(SSD_GROUPS):
            gs = slice(grp * gw, (grp + 1) * gw)
            o_ref[0, rows, gs] = _rms(y[:, gs], ng_ref[:, gs]).astype(BF16)
        return carry

    lax.fori_loop(0, n_chunks, chunk, 0, unroll=2)


def _head_expander():
    e = np.zeros((HY_SMALL, SSD_INNER), np.float32)
    for h in range(SSD_HEADS):
        e[HY_DT_LANE + h, h * SSD_HEADDIM:(h + 1) * SSD_HEADDIM] = 1.0
    return jnp.asarray(e, BF16)


def _conv_delays():
    c = SSD_CHUNK
    m = np.zeros(((SSD_CONV - 1) * c, 2 * c), np.float32)
    for k in range(SSD_CONV - 1):
        for t in range(c):
            m[k * c + t, c + t - (SSD_CONV - 1) + k] = 1.0
    return jnp.asarray(m, BF16)


def _ssd(p, sm, cwx, cbx, cwb, cbb, dtb, a_pad, d_full, ng, *, cs=512):
    bsz, s, _ = p.shape
    cs = min(cs, s)
    consts = (cwx, cbx, cwb, cbb, dtb, a_pad, d_full, ng, _head_expander(), _conv_delays())
    return pl.pallas_call(
        functools.partial(_ssd_body, cs=cs, n_chunks=cs // SSD_CHUNK),
        grid=(bsz, s // cs),
        in_specs=[
            pl.BlockSpec((1, cs, SSD_INNER), lambda b, i: (b, i, 3)),
            pl.BlockSpec((1, cs, SSD_INNER), lambda b, i: (b, i, 4)),
            pl.BlockSpec((1, cs, 2 * SSD_BC), lambda b, i: (b, i, 10)),
            pl.BlockSpec((1, cs, HY_SMALL), lambda b, i: (b, i, 0)),
        ] + [_resident(a.shape) for a in consts],
        out_specs=pl.BlockSpec((1, cs, SSD_INNER), lambda b, i: (b, i, 0)),
        out_shape=jax.ShapeDtypeStruct((bsz, s, SSD_INNER), BF16),
        scratch_shapes=[
            pltpu.VMEM((cs + SSD_CHUNK, SSD_INNER), BF16),
            pltpu.VMEM((cs + SSD_CHUNK, 2 * SSD_BC), BF16),
            pltpu.VMEM((SSD_STATE, SSD_INNER), F32),
            pltpu.VMEM((SSD_CHUNK, SSD_INNER), F32),
        ],
        compiler_params=_params("parallel", "arbitrary"),
        name="ssd",
    )(p, p, p, sm, *consts)


def _out_proj_body(x_ref, a_ref, b_ref, wa_ref, wb_ref, o_ref):
    o_ref[...] = x_ref[...] + _nn(a_ref[...], wa_ref[...]) + _nn(b_ref[...], wb_ref[...])


def _out_proj(x, a, b, wa, wb, *, tm=512):
    t, d = x.shape
    tm = min(tm, t)
    tok = lambda w: pl.BlockSpec((tm, w), lambda i: (i, 0))
    return pl.pallas_call(
        _out_proj_body,
        grid=(t // tm,),
        in_specs=[tok(d), tok(a.shape[1]), tok(b.shape[1]),
                  _resident(wa.shape), _resident(wb.shape)],
        out_specs=tok(d),
        out_shape=jax.ShapeDtypeStruct((t, d), F32),
        compiler_params=_params("parallel"),
        name="hy_out_proj",
    )(x, a, b, wa, wb)


DSA_H_WIDTH = 512
DSA_LAT_EXT = DSA_LATENT + 16


def _dsa_proj_body(x_ref, g_ref, wh_ref, wq_ref, qg_ref, kvg_ref, lng_ref, lnb_ref,
                   q_ref, qi_ref, kv_ref, kvt_ref, ki_ref, wt_ref):
    h = _rms(x_ref[...], g_ref[...]).astype(BF16)
    c = _nn(h, wh_ref[...])
    q_lat = _rms(c[:, :DSA_Q_RANK], qg_ref[...]).astype(BF16)
    qq = _nn(q_lat, wq_ref[...])
    nq = DSA_HEADS * DSA_LATENT
    q_ref[...] = (qq[:, :nq] * (DSA_LATENT ** -0.5 * LOG2E)).astype(BF16)
    qi_ref[...] = qq[:, nq:].astype(BF16)
    kv = _rms(c[:, DSA_Q_RANK:DSA_Q_RANK + DSA_LATENT], kvg_ref[...])
    kv_ref[...] = kv.astype(BF16)
    kvt_ref[0:DSA_LATENT, :] = kv.T.astype(BF16)
    ones_row = lax.broadcasted_iota(I32, (DSA_LAT_EXT - DSA_LATENT, kv.shape[0]), 0) == 0
    kvt_ref[DSA_LATENT:, :] = ones_row.astype(F32).astype(BF16)
    tail = c[:, DSA_Q_RANK + DSA_LATENT:]
    kr = tail[:, :IDX_DIM]
    mu = jnp.mean(kr, axis=-1, keepdims=True)
    var = jnp.mean(jnp.square(kr - mu), axis=-1, keepdims=True)
    ki = (kr - mu) * lax.rsqrt(var + EPS) * lng_ref[...] + lnb_ref[...]
    ki_ref[...] = ki.astype(BF16)
    tail_t = (tail * (IDX_HEADS ** -0.5 * IDX_DIM ** -0.5)).T
    wt_ref[...] = tail_t[IDX_DIM:IDX_DIM + IDX_HEADS, :]


def _dsa_proj(x, g, wh, wq, qg, kvg, lng, lnb, *, tm=512):
    t, d = x.shape
    tm = min(tm, t)
    nq = DSA_HEADS * DSA_LATENT
    ni = IDX_HEADS * IDX_DIM
    full = lambda a: _resident(a.shape)
    tok = lambda w: pl.BlockSpec((tm, w), lambda i: (i, 0))
    return pl.pallas_call(
        _dsa_proj_body,
        grid=(t // tm,),
        in_specs=[tok(d), full(g), full(wh), full(wq), full(qg), full(kvg), full(lng), full(lnb)],
        out_specs=[tok(nq), tok(ni), tok(DSA_LATENT),
                   pl.BlockSpec((DSA_LAT_EXT, tm), lambda i: (0, i)),
                   tok(IDX_DIM),
                   pl.BlockSpec((IDX_HEADS, tm), lambda i: (0, i))],
        out_shape=[
            jax.ShapeDtypeStruct((t, nq), BF16),
            jax.ShapeDtypeStruct((t, ni), BF16),
            jax.ShapeDtypeStruct((t, DSA_LATENT), BF16),
            jax.ShapeDtypeStruct((DSA_LAT_EXT, t), BF16),
            jax.ShapeDtypeStruct((t, IDX_DIM), BF16),
            jax.ShapeDtypeStruct((IDX_HEADS, t), F32),
        ],
        compiler_params=_params("parallel"),
        name="dsa_proj",
    )(x, g, wh, wq, qg, kvg, lng, lnb)


DSA_TQ = 256
PART_ROWS = 4 * SUBLANES


def _dsa_attn_body(q_ref, qi_ref, wt_ref, ki_ref, kv_ref, kvt_ref, bias_ref, x_ref, wuv_ref, wo_ref,
                   o_ref, key_scr, top_scr, s_scr, acc_scr, m_scr, bm_scr, al_scr, mb_scr, u_scr, *, topk):
    tq = DSA_TQ
    tk = DSA_TQ
    qb = pl.program_id(1)
    krow = lax.broadcasted_iota(I32, (tk, tq), 0)
    qcol = lax.broadcasted_iota(I32, (tk, tq), 1)
    causal = krow <= qcol

    def score_block(row0, n_rows, diag):
        rows = pl.ds(pl.multiple_of(row0, tk), n_rows)
        kk = ki_ref[rows, :]
        s = jnp.zeros((n_rows, tq), F32)
        for h in range(IDX_HEADS):
            z = _nt(kk, qi_ref[:, h * IDX_DIM:(h + 1) * IDX_DIM])
            s = s + jnp.maximum(z, 0.0) * wt_ref[h:h + 1, :]
        if diag:
            s = jnp.where(causal, s, -jnp.inf)
        bits = pltpu.bitcast(s, I32)
        key_scr[rows, :] = bits ^ ((bits >> 31) & 0x7FFFFFFF)
        top_scr[rows, :] = pltpu.bitcast(bits & -65536, F32).astype(BF16)

    def score_loop(kp, carry):
        score_block(kp * (2 * tk), 2 * tk, False)
        return carry

    lax.fori_loop(0, qb // 2, score_loop, 0)

    @pl.when(qb % 2 == 1)
    def _():
        score_block((qb - 1) * tk, tk, False)

    score_block(qb * tk, tk, True)
    pad_rows = pl.ds(pl.multiple_of((qb + 1) * tk, tk), tk)
    key_scr[pad_rows, :] = jnp.full((tk, tq), INT_MIN, I32)
    top_scr[pad_rows, :] = jnp.full((tk, tq), -jnp.inf, BF16)
    n_pairs = (qb + 2) // 2

    pack_rows = 4 * BF16_SUBLANES

    def top_step(it, upper):
        cand = upper | (jnp.int32(1) << (15 - it))
        bits = jnp.where(cand >= 0x8000, cand ^ 0x8000, (~cand) & 0xFFFF)
        cand_f = pltpu.bitcast(bits << 16, F32).astype(BF16)

        def count(kp, cnt):
            rows = pl.ds(pl.multiple_of(kp * (2 * tk), 2 * tk), 2 * tk)
            one = jnp.where(top_scr[rows, :] >= cand_f, jnp.ones((), BF16), jnp.zeros((), BF16))
            for i in range(2 * tk // pack_rows):
                cnt = cnt + one[i * pack_rows:(i + 1) * pack_rows]
            return cnt

        cnt = lax.fori_loop(0, n_pairs, count, jnp.zeros((pack_rows, tq), BF16))
        tot = jnp.sum(cnt.astype(F32), axis=0, keepdims=True)
        return jnp.where(tot >= topk, cand, upper)

    upper = lax.fori_loop(0, 16, top_step, jnp.zeros((1, tq), I32))

    def bit_step(it, thr):
        cand = thr ^ (jnp.int32(1) << (31 - it))

        def count(kp, cnt):
            rows = pl.ds(pl.multiple_of(kp * (2 * tk), 2 * tk), 2 * tk)
            ge = jnp.where(key_scr[rows, :] >= cand, 1, 0)
            return cnt + jnp.sum(ge.reshape(2 * tk // PART_ROWS, PART_ROWS, tq), axis=0)

        cnt = lax.fori_loop(0, n_pairs, count, jnp.zeros((PART_ROWS, tq), I32))
        tot = jnp.sum(cnt, axis=0, keepdims=True)
        return jnp.where(tot >= topk, cand, thr)

    thr = lax.fori_loop(16, 32, bit_step, (upper << 16) ^ INT_MIN)

    m_scr[...] = jnp.full(m_scr.shape, NEG_BIG, F32)
    acc_scr[...] = jnp.zeros_like(acc_scr)
    key_minus_query = krow - qcol

    def block_rows(kb):
        return pl.ds(pl.multiple_of(kb * tk, tk), tk)

    def tiles(x):
        return x.reshape(x.shape[0] // SUBLANES, SUBLANES, tq)

    def block_max(s):
        part = jnp.max(s.reshape(tk // PART_ROWS, PART_ROWS, tq), axis=0)
        m8 = jnp.max(tiles(part), axis=0)
        for shift in (4, 2, 1):
            m8 = jnp.maximum(m8, pltpu.roll(m8, shift, axis=0))
        return m8

    def attn_step(kb, cur, with_logits, with_probs):
        prev = 1 - cur
        if with_logits:
            kvb = kv_ref[block_rows(kb), :]
            sel = jnp.logical_and(key_scr[block_rows(kb), :] >= thr, key_minus_query <= (qb - kb) * tk)
            cap = jnp.where(sel, jnp.inf, NEG_BIG)
        if with_probs:
            kvtb = kvt_ref[:, block_rows(kb - 1)]
        for h in range(DSA_HEADS):
            if with_logits:
                s = _nt(kvb, q_ref[:, h * DSA_LATENT:(h + 1) * DSA_LATENT])
                s = jnp.minimum(s, cap)
                s_scr[cur, h] = tiles(s)
                bm_scr[h] = block_max(s)
            if with_probs:
                p = jnp.exp2(s_scr[prev, h] - mb_scr[prev, h][None]).reshape(tk, tq)
                acc_scr[h] = acc_scr[h] * al_scr[prev, h][None] + tiles(_nn(kvtb, p.astype(BF16)))
        if with_logits:
            @pl.when(kb >= qb - 1)
            def _():
                for h in range(DSA_HEADS):
                    s = s_scr[cur, h].reshape(tk, tq) + bias_ref[h, qb - kb]
                    s_scr[cur, h] = tiles(s)
                    bm_scr[h] = block_max(s)

            m_old = m_scr[...]
            m_new = jnp.maximum(m_old, bm_scr[...])
            al_scr[cur] = jnp.exp2(m_old - m_new)
            mb_scr[cur] = m_new
            m_scr[...] = m_new

    attn_step(0, 0, True, False)

    def attn_pair(j, carry):
        attn_step(2 * j + 1, 1, True, True)
        attn_step(2 * j + 2, 0, True, True)
        return carry

    lax.fori_loop(0, qb // 2, attn_pair, 0)

    @pl.when(qb % 2 == 1)
    def _():
        attn_step(qb, 1, True, True)
        attn_step(qb + 1, 0, False, True)

    @pl.when(qb % 2 == 0)
    def _():
        attn_step(qb + 1, 1, False, True)

    for h in range(DSA_HEADS):
        acc = acc_scr[h].reshape(DSA_LAT_EXT, tq)
        o_t = (acc[0:DSA_LATENT] / acc[DSA_LATENT:DSA_LATENT + 1]).astype(BF16)
        u_scr[h * DSA_VDIM:(h + 1) * DSA_VDIM, :] = _nn(wuv_ref[h], o_t).astype(BF16)
    y_t = _nn(wo_ref[...], u_scr[...])
    o_ref[...] = x_ref[...] + y_t.T


def _dsa_attn(x, q, qi, wt, ki, kv, kvt, bias_t, wuv_t, wo_t, *, bsz, s):
    t, d = x.shape
    tq = DSA_TQ
    nq = s // tq
    topk = min(TOPK_MAX, s // 4)
    full = lambda a: _resident(a.shape)
    return pl.pallas_call(
        functools.partial(_dsa_attn_body, topk=topk),
        grid=(bsz, nq),
        in_specs=[
            pl.BlockSpec((tq, q.shape[1]), lambda b, i: (b * nq + i, 0)),
            pl.BlockSpec((tq, qi.shape[1]), lambda b, i: (b * nq + i, 0)),
            pl.BlockSpec((IDX_HEADS, tq), lambda b, i: (0, b * nq + i)),
            pl.BlockSpec((s, IDX_DIM), lambda b, i: (b, 0)),
            pl.BlockSpec((s, DSA_LATENT), lambda b, i: (b, 0)),
            pl.BlockSpec((DSA_LAT_EXT, s), lambda b, i: (0, b)),
            full(bias_t),
            pl.BlockSpec((tq, d), lambda b, i: (b * nq + i, 0)),
            full(wuv_t), full(wo_t),
        ],
        out_specs=pl.BlockSpec((tq, d), lambda b, i: (b * nq + i, 0)),
        out_shape=jax.ShapeDtypeStruct((t, d), F32),
        scratch_shapes=[
            pltpu.VMEM((s + tq, tq), I32),
            pltpu.VMEM((s + tq, tq), BF16),
            pltpu.VMEM((2, DSA_HEADS, tq // SUBLANES, SUBLANES, tq), F32),
            pltpu.VMEM((DSA_HEADS, DSA_LAT_EXT // SUBLANES, SUBLANES, tq), F32),
            pltpu.VMEM((DSA_HEADS, SUBLANES, tq), F32),
            pltpu.VMEM((DSA_HEADS, SUBLANES, tq), F32),
            pltpu.VMEM((2, DSA_HEADS, SUBLANES, tq), F32),
            pltpu.VMEM((2, DSA_HEADS, SUBLANES, tq), F32),
            pltpu.VMEM((DSA_HEADS * DSA_VDIM, tq), BF16),
        ],
        compiler_params=_params("parallel", "arbitrary"),
        name="dsa_attn",
    )(q, qi, wt, ki, kv, kvt, bias_t, x, wuv_t, wo_t)


def _t5_bucket(rel):
    max_exact = REL_BUCKETS // 2
    relf = jnp.maximum(rel, 1).astype(F32)
    large = max_exact + (jnp.log(relf / max_exact) / math.log(REL_MAX_DIST / max_exact)
                         * (REL_BUCKETS - max_exact)).astype(I32)
    large = jnp.minimum(large, REL_BUCKETS - 1)
    return jnp.where(rel < max_exact, rel, large)


def _bias_tiles(rel_bias):
    tq = DSA_TQ
    kk = jnp.arange(tq)[:, None]
    qq = jnp.arange(tq)[None, :]
    rel = jnp.stack([qq - kk, qq - kk + tq])
    table = (rel_bias - rel_bias[REL_BUCKETS - 1][None, :]) * LOG2E
    onehot = (_t5_bucket(jnp.maximum(rel, 0))[..., None] == jnp.arange(REL_BUCKETS)).astype(F32)
    return jnp.einsum("kabn,nh->hkab", onehot, table, precision=lax.Precision.HIGHEST)


def _hy_in_weight(w_in):
    sizes = (GLA_QK, GLA_QK, GLA_V, GLA_V, GLA_GATE_RANK, SSD_INNER, SSD_INNER + 2 * SSD_BC, SSD_HEADS)
    offs = np.cumsum((0,) + sizes)
    q, k, v, r, g_lr, z, xbc, dt = [w_in[:, offs[i]:offs[i + 1]] for i in range(len(sizes))]
    pad = jnp.zeros((w_in.shape[0], HY_SMALL - GLA_GATE_RANK - SSD_HEADS), w_in.dtype)
    return jnp.concatenate([q, k, v, r, z, xbc, g_lr, dt, pad], axis=1).astype(BF16)


def kernel(x, ffn_norm_g, ffn_w_gate, ffn_w_up, ffn_w_down, mix_norm_g, hy_w_in, gla_w_gate2, gla_b_gate, gla_norm_g, ssd_conv_w, ssd_conv_b, ssd_dt_bias, ssd_a_log, ssd_d, ssd_norm_g, hy_w_out, dsa_w_dq, dsa_q_norm_g, dsa_w_uq, dsa_w_dkv, dsa_kv_norm_g, dsa_w_uv, dsa_w_o, idx_w_q, idx_w_k, idx_ln_g, idx_ln_b, idx_w_w, rel_bias, final_norm_g):
    bsz, s, d = x.shape
    t = bsz * s
    xt = x.reshape(t, d)
    row = lambda a: a.reshape(1, -1).astype(F32)

    wg_all, wu_all, wd_all = (w.astype(BF16) for w in (ffn_w_gate, ffn_w_up, ffn_w_down))

    def ffn(xt, layer, half, final):
        return _ffn(xt, row(ffn_norm_g[layer, half]), wg_all, wu_all, wd_all, row(final_norm_g),
                    layer=layer, half=half, final_norm=final)

    xt = ffn(xt, 0, 0, False)
    p, sm = _in_proj(xt, row(mix_norm_g[0]), _hy_in_weight(hy_w_in[0]))
    p = p.reshape(bsz, s, HY_MAIN)
    sm = sm.reshape(bsz, s, HY_SMALL)
    wg2 = jnp.zeros((HY_SMALL, GLA_QK), F32).at[:GLA_GATE_RANK].set(gla_w_gate2[0]).astype(BF16)
    o_gla = _gla(p, sm, wg2, row(gla_b_gate[0]), row(gla_norm_g[0]))
    cw, cb = ssd_conv_w[0].astype(F32), ssd_conv_b[0].astype(F32)
    lane_pad = lambda v: jnp.zeros((1, HY_SMALL), F32).at[0, HY_DT_LANE:HY_DT_LANE + SSD_HEADS].set(v)
    y_ssd = _ssd(p, sm, cw[:, :SSD_INNER], row(cb[:SSD_INNER]), cw[:, SSD_INNER:], row(cb[SSD_INNER:]),
                 lane_pad(ssd_dt_bias[0].astype(F32)), lane_pad(-jnp.exp(ssd_a_log[0].astype(F32))),
                 row(jnp.repeat(ssd_d[0], SSD_HEADDIM)), row(ssd_norm_g[0]))
    w_out = hy_w_out[0].astype(BF16)
    xt = _out_proj(xt, o_gla.reshape(t, GLA_V), y_ssd.reshape(t, SSD_INNER), w_out[:GLA_V], w_out[GLA_V:])
    xt = ffn(xt, 0, 1, False)

    xt = ffn(xt, 1, 0, False)
    wh = jnp.concatenate(
        [dsa_w_dq[0], dsa_w_dkv[0], idx_w_k[0], idx_w_w[0],
         jnp.zeros((d, DSA_H_WIDTH - DSA_Q_RANK - DSA_LATENT - IDX_DIM - IDX_HEADS), F32)], axis=1).astype(BF16)
    wq = jnp.concatenate([dsa_w_uq[0], idx_w_q[0]], axis=1).astype(BF16)
    q, qi, kv, kvt, ki, wt = _dsa_proj(xt, row(mix_norm_g[1]), wh, wq, row(dsa_q_norm_g[0]),
                                       row(dsa_kv_norm_g[0]), row(idx_ln_g[0]), row(idx_ln_b[0]))
    wuv_t = jnp.transpose(dsa_w_uv[0], (0, 2, 1)).astype(BF16)
    wo_t = dsa_w_o[0].T.astype(BF16)
    xt = _dsa_attn(xt, q, qi, wt, ki, kv, kvt, _bias_tiles(rel_bias.astype(F32)), wuv_t, wo_t, bsz=bsz, s=s)
    xt = ffn(xt, 1, 1, True)
    return xt.reshape(bsz, s, d)
```

```python
import functools
import math

import numpy as np
import jax
import jax.numpy as jnp
from jax import lax
from jax.experimental import pallas as pl
from jax.experimental.pallas import tpu as pltpu

F32 = jnp.float32
BF16 = jnp.bfloat16
I32 = jnp.int32

EPS = 1e-6
D_MODEL = 1024
D_FF = 2816
GLA_HEADS = 4
GLA_DK = 128
GLA_DV = 256
GLA_GATE_RANK = 16
GLA_GATE_TAU = 16.0
GLA_CHUNK = 128
GLA_QK = GLA_HEADS * GLA_DK
GLA_V = GLA_HEADS * GLA_DV
SSD_HEADS = 16
SSD_HEADDIM = 64
SSD_STATE = 128
SSD_GROUPS = 2
SSD_CONV = 4
SSD_CHUNK = 128
SSD_INNER = SSD_HEADS * SSD_HEADDIM
SSD_BC = SSD_GROUPS * SSD_STATE
SSD_HEADS_PER_GROUP = SSD_HEADS // SSD_GROUPS
DSA_HEADS = 16
DSA_Q_RANK = 256
DSA_LATENT = 128
DSA_VDIM = 64
IDX_HEADS = 16
IDX_DIM = 64
TOPK_MAX = 256
REL_BUCKETS = 32
REL_MAX_DIST = 128

LANES = 128
SUBLANES = 8
BF16_SUBLANES = 16
VMEM_LIMIT = 56 * 1024 * 1024

HY_SMALL = LANES
HY_MAIN = 2 * GLA_QK + 2 * GLA_V + SSD_INNER + SSD_INNER + 2 * SSD_BC
HY_WIDTH = HY_MAIN + HY_SMALL
HY_DT_LANE = GLA_GATE_RANK

NEG_BIG = -1e30
LOG2E = math.log2(math.e)
INT_MIN = -2 ** 31


def _nt(a, b):
    return lax.dot_general(a, b, (((1,), (1,)), ((), ())), preferred_element_type=F32)


def _tn(a, b):
    return lax.dot_general(a, b, (((0,), (0,)), ((), ())), preferred_element_type=F32)


def _nn(a, b):
    return jnp.dot(a, b, preferred_element_type=F32)


def _silu(a):
    return a / (1.0 + jnp.exp(-a))


def _softplus(a):
    return jnp.maximum(a, 0.0) + jnp.log1p(jnp.exp(-jnp.abs(a)))


def _rms(x, g):
    return x * lax.rsqrt(jnp.mean(x * x, axis=-1, keepdims=True) + EPS) * g


def _params(*sem):
    return pltpu.CompilerParams(dimension_semantics=sem, vmem_limit_bytes=VMEM_LIMIT)


def _ffn_body(x_ref, g_ref, wg_ref, wu_ref, wd_ref, fg_ref, *rest, final_norm):
    o_ref = rest[-1]
    x = x_ref[...]
    if len(rest) > 1:
        a_ref, b_ref, wa_ref, wb_ref = rest[:-1]
        x = x + _nn(a_ref[...], wa_ref[...]) + _nn(b_ref[...], wb_ref[...])
    h = _rms(x, g_ref[...]).astype(BF16)
    a = _nn(h, wg_ref[...])
    b = _nn(h, wu_ref[...])
    t = (_silu(a) * b).astype(BF16)
    y = x + 0.5 * _nn(t, wd_ref[...])
    if final_norm:
        y = _rms(y, fg_ref[...])
    o_ref[...] = y


def _resident(shape):
    return pl.BlockSpec(shape, lambda *_: (0,) * len(shape), pipeline_mode=pl.Buffered(1))


def _ffn(x, g, wg, wu, wd, fg, *, layer, half, final_norm, mix=None, tm=512):
    t, d = x.shape
    tm = min(tm, t)
    tok = lambda w: pl.BlockSpec((tm, w), lambda i: (i, 0))
    pick = lambda w: pl.BlockSpec((None, None) + w.shape[2:], lambda i: (layer, half, 0, 0),
                                  pipeline_mode=pl.Buffered(1))
    mix_specs, mix_args = [], ()
    if mix is not None:
        a, b, wa, wb = mix_args = mix
        mix_specs = [tok(a.shape[1]), tok(b.shape[1]), _resident(wa.shape), _resident(wb.shape)]
    return pl.pallas_call(
        functools.partial(_ffn_body, final_norm=final_norm),
        grid=(t // tm,),
        in_specs=[tok(d), _resident(g.shape), pick(wg), pick(wu), pick(wd), _resident(fg.shape)] + mix_specs,
        out_specs=tok(d),
        out_shape=jax.ShapeDtypeStruct((t, d), F32),
        compiler_params=_params("parallel"),
        name="ffn",
    )(x, g, wg, wu, wd, fg, *mix_args)


def _in_proj_body(x_ref, g_ref, w_ref, o_ref, sm_ref):
    h = _rms(x_ref[...], g_ref[...]).astype(BF16)
    o_ref[...] = _nn(h, w_ref[:, :HY_MAIN]).astype(BF16)
    sm_ref[...] = _nn(h, w_ref[:, HY_MAIN:])


def _in_proj(x, g, w, *, tm=512):
    t, d = x.shape
    tm = min(tm, t)
    return pl.pallas_call(
        _in_proj_body,
        grid=(t // tm,),
        in_specs=[pl.BlockSpec((tm, d), lambda i: (i, 0)), _resident(g.shape), _resident(w.shape)],
        out_specs=[pl.BlockSpec((tm, HY_MAIN), lambda i: (i, 0)), pl.BlockSpec((tm, HY_SMALL), lambda i: (i, 0))],
        out_shape=[jax.ShapeDtypeStruct((t, HY_MAIN), BF16), jax.ShapeDtypeStruct((t, HY_SMALL), F32)],
        compiler_params=_params("parallel"),
        name="hy_in_proj",
    )(x, g, w)


def _gla_body(q_ref, k_ref, v_ref, r_ref, sm_ref, wg2_ref, bg_ref, ng_ref, o_ref,
              st_scr, qe_scr, qm_scr, km_scr, kd_scr, el_scr, *, n_chunks):
    @pl.when(pl.program_id(1) == 0)
    def _():
        st_scr[...] = jnp.zeros_like(st_scr)

    c = GLA_CHUNK
    row = lax.broadcasted_iota(I32, (c, c), 0)
    col = lax.broadcasted_iota(I32, (c, c), 1)
    tril = row >= col
    tri_f = tril.astype(F32)

    gl = _nn(sm_ref[0].astype(BF16), wg2_ref[...]) + bg_ref[...]
    g = -_softplus(-gl) * (1.0 / GLA_GATE_TAU)
    q = q_ref[0].astype(F32) * (GLA_DK ** -0.5)
    k = k_ref[0].astype(F32)
    for ci in range(n_chunks):
        rs = slice(ci * c, (ci + 1) * c)
        b = jnp.dot(tri_f, g[rs], preferred_element_type=F32, precision=lax.Precision.HIGHEST)
        b_last = b[c - 1:c, :]
        b_mid = b[c // 2 - 1:c // 2, :]
        qe_scr[rs, :] = (q[rs] * jnp.exp(b)).astype(BF16)
        qm_scr[rs, :] = (q[rs] * jnp.exp(b - b_mid)).astype(BF16)
        km_scr[rs, :] = (k[rs] * jnp.exp(b_mid - b)).astype(BF16)
        kd_scr[rs, :] = (k[rs] * jnp.exp(b_last - b)).astype(BF16)
        el_scr[ci] = jnp.broadcast_to(jnp.exp(b_last), (SUBLANES, GLA_QK))

    def chunk(ci, carry):
        rows = pl.ds(pl.multiple_of(ci * c, c), c)
        e_last = el_scr[ci][0:1]
        for h in range(GLA_HEADS):
            ks = slice(h * GLA_DK, (h + 1) * GLA_DK)
            vs = slice(h * GLA_DV, (h + 1) * GLA_DV)
            v = v_ref[0, rows, vs]
            st = st_scr[h]
            attn = jnp.where(tril, _nt(qm_scr[rows, ks], km_scr[rows, ks]), 0.0).astype(BF16)
            o = _nt(qe_scr[rows, ks], st.astype(BF16)) + _nn(attn, v)
            st_scr[h] = st * e_last[:, ks] + _tn(v, kd_scr[rows, ks])
            o = _rms(o, ng_ref[:, vs])
            o_ref[0, rows, vs] = (o * _silu(r_ref[0, rows, vs].astype(F32))).astype(BF16)
        return carry

    lax.fori_loop(0, n_chunks, chunk, 0, unroll=2)


def _gla(p, sm, wg2, bg, ng, *, cs=512):
    bsz, s, _ = p.shape
    cs = min(cs, s)
    return pl.pallas_call(
        functools.partial(_gla_body, n_chunks=cs // GLA_CHUNK),
        grid=(bsz, s // cs),
        in_specs=[
            pl.BlockSpec((1, cs, GLA_QK), lambda b, i: (b, i, 0)),
            pl.BlockSpec((1, cs, GLA_QK), lambda b, i: (b, i, 1)),
            pl.BlockSpec((1, cs, GLA_V), lambda b, i: (b, i, 1)),
            pl.BlockSpec((1, cs, GLA_V), lambda b, i: (b, i, 2)),
            pl.BlockSpec((1, cs, HY_SMALL), lambda b, i: (b, i, 0)),
            _resident(wg2.shape), _resident(bg.shape), _resident(ng.shape),
        ],
        out_specs=pl.BlockSpec((1, cs, GLA_V), lambda b, i: (b, i, 0)),
        out_shape=jax.ShapeDtypeStruct((bsz, s, GLA_V), BF16),
        scratch_shapes=[pltpu.VMEM((GLA_HEADS, GLA_DV, GLA_DK), F32)]
        + [pltpu.VMEM((cs, GLA_QK), BF16)] * 4
        + [pltpu.VMEM((cs // GLA_CHUNK, SUBLANES, GLA_QK), F32)],
        compiler_params=_params("parallel", "arbitrary"),
        name="gla",
    )(p, p, p, p, sm, wg2, bg, ng)


def _ssd_body(z_ref, xs_ref, bc_ref, sm_ref, cwx_ref, cbx_ref, cwb_ref, cbb_ref, dtb_ref, a_ref, d_ref,
              ng_ref, exb_ref, shift_ref, o_ref, xpad, bpad, st_scr, y_scr, *, cs, n_chunks):
    first = pl.program_id(1) == 0
    c = SSD_CHUNK
    halo = c

    @pl.when(first)
    def _():
        st_scr[...] = jnp.zeros_like(st_scr)
        xpad[0:halo, :] = jnp.zeros((halo, SSD_INNER), BF16)
        bpad[0:halo, :] = jnp.zeros((halo, 2 * SSD_BC), BF16)

    @pl.when(jnp.logical_not(first))
    def _():
        xpad[0:halo, :] = xpad[cs:cs + halo, :]
        bpad[0:halo, :] = bpad[cs:cs + halo, :]

    xpad[halo:halo + cs, :] = xs_ref[0]
    bpad[halo:halo + cs, :] = bc_ref[0]

    row = lax.broadcasted_iota(I32, (c, c), 0)
    col = lax.broadcasted_iota(I32, (c, c), 1)
    tril = row >= col
    tri_f = tril.astype(F32)
    lo_half = lax.broadcasted_iota(I32, (c, 2 * SSD_HEADDIM), 1) < SSD_HEADDIM
    gw = SSD_INNER // SSD_GROUPS

    def conv(pad_ref, w_ref, b_ref, start):
        win = pad_ref[pl.ds(start, 2 * c), :]
        delayed = _nn(shift_ref[...], win)
        acc = b_ref[...] + w_ref[SSD_CONV - 1:SSD_CONV, :] * win[c:2 * c, :].astype(F32)
        for kk in range(SSD_CONV - 1):
            acc = acc + w_ref[kk:kk + 1, :] * delayed[kk * c:(kk + 1) * c, :]
        return _silu(acc)

    def spread_exact(v):
        hi = v.astype(BF16)
        rest = v - hi.astype(F32)
        mid = rest.astype(BF16)
        lo = (rest - mid.astype(F32)).astype(BF16)
        pieces = jnp.concatenate([jnp.broadcast_to(p, (BF16_SUBLANES, HY_SMALL)) for p in (hi, mid, lo)], axis=0)
        out = _nn(pieces, exb_ref[...])
        return out[0:1] + out[BF16_SUBLANES:BF16_SUBLANES + 1] + out[2 * BF16_SUBLANES:2 * BF16_SUBLANES + 1]

    def chunk(ci, carry):
        start = pl.multiple_of(ci * c, c)
        rows = pl.ds(start, c)
        xc = conv(xpad, cwx_ref, cbx_ref, start)
        bcc = conv(bpad, cwb_ref, cbb_ref, start)
        dt = _softplus(sm_ref[0, rows, :] + dtb_ref[...])
        da = dt * a_ref[...]
        acum = jnp.dot(tri_f, da, preferred_element_type=F32, precision=lax.Precision.HIGHEST)
        acum_t = acum.T
        a_last = acum[c - 1:c, :]
        fac = jnp.concatenate([dt, jnp.exp(acum), jnp.exp(a_last - acum)], axis=0).astype(BF16)
        fac_x = _nn(fac, exb_ref[...])
        dt_x, ea_x, w_x = fac_x[0:c], fac_x[c:2 * c], fac_x[2 * c:3 * c]
        dec_x = spread_exact(jnp.exp(a_last))
        xdt = xc * dt_x
        xdt_b = xdt.astype(BF16)
        xdtw_b = (xdt * w_x).astype(BF16)
        for grp in range(SSD_GROUPS):
            gs = slice(grp * gw, (grp + 1) * gw)
            bm = bcc[:, grp * SSD_STATE:(grp + 1) * SSD_STATE].astype(BF16)
            cm = bcc[:, SSD_BC + grp * SSD_STATE:SSD_BC + (grp + 1) * SSD_STATE].astype(BF16)
            scores = _nt(cm, bm)
            st = st_scr[:, gs]
            y_inter = _nn(cm, st.astype(BF16)) * ea_x[:, gs]
            st_scr[:, gs] = st * dec_x[:, gs] + _tn(bm, xdtw_b[:, gs])
            for pair in range(SSD_HEADS_PER_GROUP // 2):
                h0 = grp * SSD_HEADS_PER_GROUP + 2 * pair
                ps = slice(h0 * SSD_HEADDIM, (h0 + 2) * SSD_HEADDIM)
                lhs = []
                for h in (h0, h0 + 1):
                    ln = HY_DT_LANE + h
                    seg = jnp.where(tril, jnp.exp(acum[:, ln:ln + 1] - acum_t[ln:ln + 1, :]), 0.0)
                    lhs.append((scores * seg).astype(BF16))
                slab = xdt_b[:, ps]
                zero = jnp.zeros_like(slab)
                rhs = jnp.concatenate([jnp.where(lo_half, slab, zero), jnp.where(lo_half, zero, slab)], axis=0)
                y_pair = _nn(jnp.concatenate(lhs, axis=1), rhs)
                y_scr[:, ps] = (y_pair + y_inter[:, 2 * pair * SSD_HEADDIM:(2 * pair + 2) * SSD_HEADDIM]
                                + d_ref[:, ps] * xc[:, ps])
        y = y_scr[...] * _silu(z_ref[0, rows, :].astype(F32))
        for grp in range(SSD_GROUPS):
            gs = slice(grp * gw, (grp + 1) * gw)
            o_ref[0, rows, gs] = _rms(y[:, gs], ng_ref[:, gs]).astype(BF16)
        return carry

    lax.fori_loop(0, n_chunks, chunk, 0, unroll=2)


def _head_expander():
    e = np.zeros((HY_SMALL, SSD_INNER), np.float32)
    for h in range(SSD_HEADS):
        e[HY_DT_LANE + h, h * SSD_HEADDIM:(h + 1) * SSD_HEADDIM] = 1.0
    return jnp.asarray(e, BF16)


def _conv_delays():
    c = SSD_CHUNK
    m = np.zeros(((SSD_CONV - 1) * c, 2 * c), np.float32)
    for k in range(SSD_CONV - 1):
        for t in range(c):
            m[k * c + t, c + t - (SSD_CONV - 1) + k] = 1.0
    return jnp.asarray(m, BF16)


def _ssd(p, sm, cwx, cbx, cwb, cbb, dtb, a_pad, d_full, ng, *, cs=512):
    bsz, s, _ = p.shape
    cs = min(cs, s)
    consts = (cwx, cbx, cwb, cbb, dtb, a_pad, d_full, ng, _head_expander(), _conv_delays())
    return pl.pallas_call(
        functools.partial(_ssd_body, cs=cs, n_chunks=cs // SSD_CHUNK),
        grid=(bsz, s // cs),
        in_specs=[
            pl.BlockSpec((1, cs, SSD_INNER), lambda b, i: (b, i, 3)),
            pl.BlockSpec((1, cs, SSD_INNER), lambda b, i: (b, i, 4)),
            pl.BlockSpec((1, cs, 2 * SSD_BC), lambda b, i: (b, i, 10)),
            pl.BlockSpec((1, cs, HY_SMALL), lambda b, i: (b, i, 0)),
        ] + [_resident(a.shape) for a in consts],
        out_specs=pl.BlockSpec((1, cs, SSD_INNER), lambda b, i: (b, i, 0)),
        out_shape=jax.ShapeDtypeStruct((bsz, s, SSD_INNER), BF16),
        scratch_shapes=[
            pltpu.VMEM((cs + SSD_CHUNK, SSD_INNER), BF16),
            pltpu.VMEM((cs + SSD_CHUNK, 2 * SSD_BC), BF16),
            pltpu.VMEM((SSD_STATE, SSD_INNER), F32),
            pltpu.VMEM((SSD_CHUNK, SSD_INNER), F32),
        ],
        compiler_params=_params("parallel", "arbitrary"),
        name="ssd",
    )(p, p, p, sm, *consts)


DSA_H_WIDTH = 512
DSA_LAT_EXT = DSA_LATENT + 16


def _dsa_proj_body(x_ref, g_ref, wh_ref, wq_ref, qg_ref, kvg_ref, lng_ref, lnb_ref,
                   q_ref, qi_ref, kv_ref, kvt_ref, ki_ref, wt_ref):
    h = _rms(x_ref[...], g_ref[...]).astype(BF16)
    c = _nn(h, wh_ref[...])
    q_lat = _rms(c[:, :DSA_Q_RANK], qg_ref[...]).astype(BF16)
    qq = _nn(q_lat, wq_ref[...])
    nq = DSA_HEADS * DSA_LATENT
    q_ref[...] = (qq[:, :nq] * (DSA_LATENT ** -0.5 * LOG2E)).astype(BF16)
    qi_ref[...] = qq[:, nq:].astype(BF16)
    kv = _rms(c[:, DSA_Q_RANK:DSA_Q_RANK + DSA_LATENT], kvg_ref[...])
    kv_ref[...] = kv.astype(BF16)
    kvt_ref[0:DSA_LATENT, :] = kv.T.astype(BF16)
    ones_row = lax.broadcasted_iota(I32, (DSA_LAT_EXT - DSA_LATENT, kv.shape[0]), 0) == 0
    kvt_ref[DSA_LATENT:, :] = ones_row.astype(F32).astype(BF16)
    tail = c[:, DSA_Q_RANK + DSA_LATENT:]
    kr = tail[:, :IDX_DIM]
    mu = jnp.mean(kr, axis=-1, keepdims=True)
    var = jnp.mean(jnp.square(kr - mu), axis=-1, keepdims=True)
    ki = (kr - mu) * lax.rsqrt(var + EPS) * lng_ref[...] + lnb_ref[...]
    ki_ref[...] = ki.astype(BF16)
    tail_t = (tail * (IDX_HEADS ** -0.5 * IDX_DIM ** -0.5)).T
    wt_ref[...] = tail_t[IDX_DIM:IDX_DIM + IDX_HEADS, :]


def _dsa_proj(x, g, wh, wq, qg, kvg, lng, lnb, *, tm=512):
    t, d = x.shape
    tm = min(tm, t)
    nq = DSA_HEADS * DSA_LATENT
    ni = IDX_HEADS * IDX_DIM
    full = lambda a: _resident(a.shape)
    tok = lambda w: pl.BlockSpec((tm, w), lambda i: (i, 0))
    return pl.pallas_call(
        _dsa_proj_body,
        grid=(t // tm,),
        in_specs=[tok(d), full(g), full(wh), full(wq), full(qg), full(kvg), full(lng), full(lnb)],
        out_specs=[tok(nq), tok(ni), tok(DSA_LATENT),
                   pl.BlockSpec((DSA_LAT_EXT, tm), lambda i: (0, i)),
                   tok(IDX_DIM),
                   pl.BlockSpec((IDX_HEADS, tm), lambda i: (0, i))],
        out_shape=[
            jax.ShapeDtypeStruct((t, nq), BF16),
            jax.ShapeDtypeStruct((t, ni), BF16),
            jax.ShapeDtypeStruct((t, DSA_LATENT), BF16),
            jax.ShapeDtypeStruct((DSA_LAT_EXT, t), BF16),
            jax.ShapeDtypeStruct((t, IDX_DIM), BF16),
            jax.ShapeDtypeStruct((IDX_HEADS, t), F32),
        ],
        compiler_params=_params("parallel"),
        name="dsa_proj",
    )(x, g, wh, wq, qg, kvg, lng, lnb)


DSA_TQ = 256
PART_ROWS = 4 * SUBLANES


def _dsa_attn_body(q_ref, qi_ref, wt_ref, ki_ref, kv_ref, kvt_ref, bias_ref, x_ref, wuv_ref, wo_ref,
                   o_ref, key_scr, top_scr, s_scr, acc_scr, m_scr, bm_scr, al_scr, mb_scr, u_scr, *, topk):
    tq = DSA_TQ
    tk = DSA_TQ
    qb = pl.program_id(1)
    krow = lax.broadcasted_iota(I32, (tk, tq), 0)
    qcol = lax.broadcasted_iota(I32, (tk, tq), 1)
    causal = krow <= qcol

    def score_block(row0, n_rows, diag):
        rows = pl.ds(pl.multiple_of(row0, tk), n_rows)
        kk = ki_ref[rows, :]
        s = jnp.zeros((n_rows, tq), F32)
        for h in range(IDX_HEADS):
            z = _nt(kk, qi_ref[:, h * IDX_DIM:(h + 1) * IDX_DIM])
            s = s + jnp.maximum(z, 0.0) * wt_ref[h:h + 1, :]
        if diag:
            s = jnp.where(causal, s, -jnp.inf)
        bits = pltpu.bitcast(s, I32)
        key_scr[rows, :] = bits ^ ((bits >> 31) & 0x7FFFFFFF)
        top_scr[rows, :] = pltpu.bitcast(bits & -65536, F32).astype(BF16)

    def score_loop(kp, carry):
        score_block(kp * (2 * tk), 2 * tk, False)
        return carry

    lax.fori_loop(0, qb // 2, score_loop, 0)

    @pl.when(qb % 2 == 1)
    def _():
        score_block((qb - 1) * tk, tk, False)

    score_block(qb * tk, tk, True)
    pad_rows = pl.ds(pl.multiple_of((qb + 1) * tk, tk), tk)
    key_scr[pad_rows, :] = jnp.full((tk, tq), INT_MIN, I32)
    top_scr[pad_rows, :] = jnp.full((tk, tq), -jnp.inf, BF16)
    n_pairs = (qb + 2) // 2

    pack_rows = 4 * BF16_SUBLANES

    def top_step(it, upper):
        cand = upper | (jnp.int32(1) << (15 - it))
        bits = jnp.where(cand >= 0x8000, cand ^ 0x8000, (~cand) & 0xFFFF)
        cand_f = pltpu.bitcast(bits << 16, F32).astype(BF16)

        def count(kp, cnt):
            rows = pl.ds(pl.multiple_of(kp * (2 * tk), 2 * tk), 2 * tk)
            one = jnp.where(top_scr[rows, :] >= cand_f, jnp.ones((), BF16), jnp.zeros((), BF16))
            for i in range(2 * tk // pack_rows):
                cnt = cnt + one[i * pack_rows:(i + 1) * pack_rows]
            return cnt

        cnt = lax.fori_loop(0, n_pairs, count, jnp.zeros((pack_rows, tq), BF16))
        tot = jnp.sum(cnt.astype(F32), axis=0, keepdims=True)
        return jnp.where(tot >= topk, cand, upper)

    upper = lax.fori_loop(0, 16, top_step, jnp.zeros((1, tq), I32))

    def bit_step(it, thr):
        cand = thr ^ (jnp.int32(1) << (31 - it))

        def count(kp, cnt):
            rows = pl.ds(pl.multiple_of(kp * (2 * tk), 2 * tk), 2 * tk)
            ge = jnp.where(key_scr[rows, :] >= cand, 1, 0)
            return cnt + jnp.sum(ge.reshape(2 * tk // PART_ROWS, PART_ROWS, tq), axis=0)

        cnt = lax.fori_loop(0, n_pairs, count, jnp.zeros((PART_ROWS, tq), I32))
        tot = jnp.sum(cnt, axis=0, keepdims=True)
        return jnp.where(tot >= topk, cand, thr)

    thr = lax.fori_loop(16, 32, bit_step, (upper << 16) ^ INT_MIN)

    m_scr[...] = jnp.full(m_scr.shape, NEG_BIG, F32)
    acc_scr[...] = jnp.zeros_like(acc_scr)
    key_minus_query = krow - qcol

    def block_rows(kb):
        return pl.ds(pl.multiple_of(kb * tk, tk), tk)

    def tiles(x):
        return x.reshape(x.shape[0] // SUBLANES, SUBLANES, tq)

    def block_max(s):
        part = jnp.max(s.reshape(tk // PART_ROWS, PART_ROWS, tq), axis=0)
        m8 = jnp.max(tiles(part), axis=0)
        for shift in (4, 2, 1):
            m8 = jnp.maximum(m8, pltpu.roll(m8, shift, axis=0))
        return m8

    def attn_step(kb, cur, with_logits, with_probs):
        prev = 1 - cur
        if with_logits:
            kvb = kv_ref[block_rows(kb), :]
            sel = jnp.logical_and(key_scr[block_rows(kb), :] >= thr, key_minus_query <= (qb - kb) * tk)
            cap = jnp.where(sel, jnp.inf, NEG_BIG)
        if with_probs:
            kvtb = kvt_ref[:, block_rows(kb - 1)]
        for h in range(DSA_HEADS):
            if with_logits:
                s = _nt(kvb, q_ref[:, h * DSA_LATENT:(h + 1) * DSA_LATENT])
                s = jnp.minimum(s, cap)
                s_scr[cur, h] = tiles(s)
                bm_scr[h] = block_max(s)
            if with_probs:
                p = jnp.exp2(s_scr[prev, h] - mb_scr[prev, h][None]).reshape(tk, tq)
                acc_scr[h] = acc_scr[h] * al_scr[prev, h][None] + tiles(_nn(kvtb, p.astype(BF16)))
        if with_logits:
            @pl.when(kb >= qb - 1)
            def _():
                for h in range(DSA_HEADS):
                    s = s_scr[cur, h].reshape(tk, tq) + bias_ref[h, qb - kb]
                    s_scr[cur, h] = tiles(s)
                    bm_scr[h] = block_max(s)

            m_old = m_scr[...]
            m_new = jnp.maximum(m_old, bm_scr[...])
            al_scr[cur] = jnp.exp2(m_old - m_new)
            mb_scr[cur] = m_new
            m_scr[...] = m_new

    attn_step(0, 0, True, False)

    def attn_pair(j, carry):
        attn_step(2 * j + 1, 1, True, True)
        attn_step(2 * j + 2, 0, True, True)
        return carry

    lax.fori_loop(0, qb // 2, attn_pair, 0)

    @pl.when(qb % 2 == 1)
    def _():
        attn_step(qb, 1, True, True)
        attn_step(qb + 1, 0, False, True)

    @pl.when(qb % 2 == 0)
    def _():
        attn_step(qb + 1, 1, False, True)

    for h in range(DSA_HEADS):
        acc = acc_scr[h].reshape(DSA_LAT_EXT, tq)
        o_t = (acc[0:DSA_LATENT] / acc[DSA_LATENT:DSA_LATENT + 1]).astype(BF16)
        u_scr[h * DSA_VDIM:(h + 1) * DSA_VDIM, :] = _nn(wuv_ref[h], o_t).astype(BF16)
    y_t = _nn(wo_ref[...], u_scr[...])
    o_ref[...] = x_ref[...] + y_t.T


def _dsa_attn(x, q, qi, wt, ki, kv, kvt, bias_t, wuv_t, wo_t, *, bsz, s):
    t, d = x.shape
    tq = DSA_TQ
    nq = s // tq
    topk = min(TOPK_MAX, s // 4)
    full = lambda a: _resident(a.shape)
    return pl.pallas_call(
        functools.partial(_dsa_attn_body, topk=topk),
        grid=(bsz, nq),
        in_specs=[
            pl.BlockSpec((tq, q.shape[1]), lambda b, i: (b * nq + i, 0)),
            pl.BlockSpec((tq, qi.shape[1]), lambda b, i: (b * nq + i, 0)),
            pl.BlockSpec((IDX_HEADS, tq), lambda b, i: (0, b * nq + i)),
            pl.BlockSpec((s, IDX_DIM), lambda b, i: (b, 0)),
            pl.BlockSpec((s, DSA_LATENT), lambda b, i: (b, 0)),
            pl.BlockSpec((DSA_LAT_EXT, s), lambda b, i: (0, b)),
            full(bias_t),
            pl.BlockSpec((tq, d), lambda b, i: (b * nq + i, 0)),
            full(wuv_t), full(wo_t),
        ],
        out_specs=pl.BlockSpec((tq, d), lambda b, i: (b * nq + i, 0)),
        out_shape=jax.ShapeDtypeStruct((t, d), F32),
        scratch_shapes=[
            pltpu.VMEM((s + tq, tq), I32),
            pltpu.VMEM((s + tq, tq), BF16),
            pltpu.VMEM((2, DSA_HEADS, tq // SUBLANES, SUBLANES, tq), F32),
            pltpu.VMEM((DSA_HEADS, DSA_LAT_EXT // SUBLANES, SUBLANES, tq), F32),
            pltpu.VMEM((DSA_HEADS, SUBLANES, tq), F32),
            pltpu.VMEM((DSA_HEADS, SUBLANES, tq), F32),
            pltpu.VMEM((2, DSA_HEADS, SUBLANES, tq), F32),
            pltpu.VMEM((2, DSA_HEADS, SUBLANES, tq), F32),
            pltpu.VMEM((DSA_HEADS * DSA_VDIM, tq), BF16),
        ],
        compiler_params=_params("parallel", "arbitrary"),
        name="dsa_attn",
    )(q, qi, wt, ki, kv, kvt, bias_t, x, wuv_t, wo_t)


def _t5_bucket(rel):
    max_exact = REL_BUCKETS // 2
    relf = jnp.maximum(rel, 1).astype(F32)
    large = max_exact + (jnp.log(relf / max_exact) / math.log(REL_MAX_DIST / max_exact)
                         * (REL_BUCKETS - max_exact)).astype(I32)
    large = jnp.minimum(large, REL_BUCKETS - 1)
    return jnp.where(rel < max_exact, rel, large)


def _bias_tiles(rel_bias):
    tq = DSA_TQ
    kk = jnp.arange(tq)[:, None]
    qq = jnp.arange(tq)[None, :]
    rel = jnp.stack([qq - kk, qq - kk + tq])
    table = (rel_bias - rel_bias[REL_BUCKETS - 1][None, :]) * LOG2E
    onehot = (_t5_bucket(jnp.maximum(rel, 0))[..., None] == jnp.arange(REL_BUCKETS)).astype(F32)
    return jnp.einsum("kabn,nh->hkab", onehot, table, precision=lax.Precision.HIGHEST)


def _hy_in_weight(w_in):
    sizes = (GLA_QK, GLA_QK, GLA_V, GLA_V, GLA_GATE_RANK, SSD_INNER, SSD_INNER + 2 * SSD_BC, SSD_HEADS)
    offs = np.cumsum((0,) + sizes)
    q, k, v, r, g_lr, z, xbc, dt = [w_in[:, offs[i]:offs[i + 1]] for i in range(len(sizes))]
    pad = jnp.zeros((w_in.shape[0], HY_SMALL - GLA_GATE_RANK - SSD_HEADS), w_in.dtype)
    return jnp.concatenate([q, k, v, r, z, xbc, g_lr, dt, pad], axis=1).astype(BF16)


def kernel(x, ffn_norm_g, ffn_w_gate, ffn_w_up, ffn_w_down, mix_norm_g, hy_w_in, gla_w_gate2, gla_b_gate, gla_norm_g, ssd_conv_w, ssd_conv_b, ssd_dt_bias, ssd_a_log, ssd_d, ssd_norm_g, hy_w_out, dsa_w_dq, dsa_q_norm_g, dsa_w_uq, dsa_w_dkv, dsa_kv_norm_g, dsa_w_uv, dsa_w_o, idx_w_q, idx_w_k, idx_ln_g, idx_ln_b, idx_w_w, rel_bias, final_norm_g):
    bsz, s, d = x.shape
    t = bsz * s
    xt = x.reshape(t, d)
    row = lambda a: a.reshape(1, -1).astype(F32)

    wg_all, wu_all, wd_all = (w.astype(BF16) for w in (ffn_w_gate, ffn_w_up, ffn_w_down))

    def ffn(xt, layer, half, final, mix=None):
        return _ffn(xt, row(ffn_norm_g[layer, half]), wg_all, wu_all, wd_all, row(final_norm_g),
                    layer=layer, half=half, final_norm=final, mix=mix)

    xt = ffn(xt, 0, 0, False)
    p, sm = _in_proj(xt, row(mix_norm_g[0]), _hy_in_weight(hy_w_in[0]))
    p = p.reshape(bsz, s, HY_MAIN)
    sm = sm.reshape(bsz, s, HY_SMALL)
    wg2 = jnp.zeros((HY_SMALL, GLA_QK), F32).at[:GLA_GATE_RANK].set(gla_w_gate2[0]).astype(BF16)
    o_gla = _gla(p, sm, wg2, row(gla_b_gate[0]), row(gla_norm_g[0]))
    cw, cb = ssd_conv_w[0].astype(F32), ssd_conv_b[0].astype(F32)
    lane_pad = lambda v: jnp.zeros((1, HY_SMALL), F32).at[0, HY_DT_LANE:HY_DT_LANE + SSD_HEADS].set(v)
    y_ssd = _ssd(p, sm, cw[:, :SSD_INNER], row(cb[:SSD_INNER]), cw[:, SSD_INNER:], row(cb[SSD_INNER:]),
                 lane_pad(ssd_dt_bias[0].astype(F32)), lane_pad(-jnp.exp(ssd_a_log[0].astype(F32))),
                 row(jnp.repeat(ssd_d[0], SSD_HEADDIM)), row(ssd_norm_g[0]))
    w_out = hy_w_out[0].astype(BF16)
    xt = ffn(xt, 0, 1, False,
             mix=(o_gla.reshape(t, GLA_V), y_ssd.reshape(t, SSD_INNER), w_out[:GLA_V], w_out[GLA_V:]))

    xt = ffn(xt, 1, 0, False)
    wh = jnp.concatenate(
        [dsa_w_dq[0], dsa_w_dkv[0], idx_w_k[0], idx_w_w[0],
         jnp.zeros((d, DSA_H_WIDTH - DSA_Q_RANK - DSA_LATENT - IDX_DIM - IDX_HEADS), F32)], axis=1).astype(BF16)
    wq = jnp.concatenate([dsa_w_uq[0], idx_w_q[0]], axis=1).astype(BF16)
    q, qi, kv, kvt, ki, wt = _dsa_proj(xt, row(mix_norm_g[1]), wh, wq, row(dsa_q_norm_g[0]),
                                       row(dsa_kv_norm_g[0]), row(idx_ln_g[0]), row(idx_ln_b[0]))
    wuv_t = jnp.transpose(dsa_w_uv[0], (0, 2, 1)).astype(BF16)
    wo_t = dsa_w_o[0].T.astype(BF16)
    xt = _dsa_attn(xt, q, qi, wt, ki, kv, kvt, _bias_tiles(rel_bias.astype(F32)), wuv_t, wo_t, bsz=bsz, s=s)
    xt = ffn(xt, 1, 1, True)
    return xt.reshape(bsz, s, d)
```

```python
import functools
import math

import numpy as np
import jax
import jax.numpy as jnp
from jax import lax
from jax.experimental import pallas as pl
from jax.experimental.pallas import tpu as pltpu

F32 = jnp.float32
BF16 = jnp.bfloat16
I32 = jnp.int32
I16 = jnp.int16
I16_MIN = -2 ** 15
I16_MAX = 2 ** 15 - 1

EPS = 1e-6
D_MODEL = 1024
D_FF = 2816
GLA_HEADS = 4
GLA_DK = 128
GLA_DV = 256
GLA_GATE_RANK = 16
GLA_GATE_TAU = 16.0
GLA_CHUNK = 128
GLA_QK = GLA_HEADS * GLA_DK
GLA_V = GLA_HEADS * GLA_DV
SSD_HEADS = 16
SSD_HEADDIM = 64
SSD_STATE = 128
SSD_GROUPS = 2
SSD_CONV = 4
SSD_CHUNK = 128
SSD_INNER = SSD_HEADS * SSD_HEADDIM
SSD_BC = SSD_GROUPS * SSD_STATE
SSD_HEADS_PER_GROUP = SSD_HEADS // SSD_GROUPS
DSA_HEADS = 16
DSA_Q_RANK = 256
DSA_LATENT = 128
DSA_VDIM = 64
IDX_HEADS = 16
IDX_DIM = 64
TOPK_MAX = 256
REL_BUCKETS = 32
REL_MAX_DIST = 128

LANES = 128
SUBLANES = 8
BF16_SUBLANES = 16
VMEM_LIMIT = 56 * 1024 * 1024

HY_SMALL = LANES
HY_MAIN = 2 * GLA_QK + 2 * GLA_V + SSD_INNER + SSD_INNER + 2 * SSD_BC
HY_WIDTH = HY_MAIN + HY_SMALL
HY_DT_LANE = GLA_GATE_RANK

NEG_BIG = -1e30
LOG2E = math.log2(math.e)
INT_MIN = -2 ** 31


def _nt(a, b):
    return lax.dot_general(a, b, (((1,), (1,)), ((), ())), preferred_element_type=F32)


def _tn(a, b):
    return lax.dot_general(a, b, (((0,), (0,)), ((), ())), preferred_element_type=F32)


def _nn(a, b):
    return jnp.dot(a, b, preferred_element_type=F32)


def _silu(a):
    return a / (1.0 + jnp.exp(-a))


def _softplus(a):
    return jnp.maximum(a, 0.0) + jnp.log1p(jnp.exp(-jnp.abs(a)))


def _rms(x, g):
    return x * lax.rsqrt(jnp.mean(x * x, axis=-1, keepdims=True) + EPS) * g


def _params(*sem):
    return pltpu.CompilerParams(dimension_semantics=sem, vmem_limit_bytes=VMEM_LIMIT)


def _ffn_body(x_ref, g_ref, wg_ref, wu_ref, wd_ref, fg_ref, *rest, final_norm):
    o_ref = rest[-1]
    x = x_ref[...]
    if len(rest) > 1:
        a_ref, b_ref, wa_ref, wb_ref = rest[:-1]
        x = x + _nn(a_ref[...], wa_ref[...]) + _nn(b_ref[...], wb_ref[...])
    h = _rms(x, g_ref[...]).astype(BF16)
    a = _nn(h, wg_ref[...])
    b = _nn(h, wu_ref[...])
    t = (_silu(a) * b).astype(BF16)
    y = x + 0.5 * _nn(t, wd_ref[...])
    if final_norm:
        y = _rms(y, fg_ref[...])
    o_ref[...] = y


def _resident(shape):
    return pl.BlockSpec(shape, lambda *_: (0,) * len(shape), pipeline_mode=pl.Buffered(1))


def _ffn(x, g, wg, wu, wd, fg, *, layer, half, final_norm, mix=None, tm=512):
    t, d = x.shape
    tm = min(tm, t)
    tok = lambda w: pl.BlockSpec((tm, w), lambda i: (i, 0))
    pick = lambda w: pl.BlockSpec((None, None) + w.shape[2:], lambda i: (layer, half, 0, 0),
                                  pipeline_mode=pl.Buffered(1))
    mix_specs, mix_args = [], ()
    if mix is not None:
        a, b, wa, wb = mix_args = mix
        mix_specs = [tok(a.shape[1]), tok(b.shape[1]), _resident(wa.shape), _resident(wb.shape)]
    return pl.pallas_call(
        functools.partial(_ffn_body, final_norm=final_norm),
        grid=(t // tm,),
        in_specs=[tok(d), _resident(g.shape), pick(wg), pick(wu), pick(wd), _resident(fg.shape)] + mix_specs,
        out_specs=tok(d),
        out_shape=jax.ShapeDtypeStruct((t, d), F32),
        compiler_params=_params("parallel"),
        name="ffn",
    )(x, g, wg, wu, wd, fg, *mix_args)


def _in_proj_body(x_ref, g_ref, w_ref, o_ref, sm_ref):
    h = _rms(x_ref[...], g_ref[...]).astype(BF16)
    o_ref[...] = _nn(h, w_ref[:, :HY_MAIN]).astype(BF16)
    sm_ref[...] = _nn(h, w_ref[:, HY_MAIN:])


def _in_proj(x, g, w, *, tm=512):
    t, d = x.shape
    tm = min(tm, t)
    return pl.pallas_call(
        _in_proj_body,
        grid=(t // tm,),
        in_specs=[pl.BlockSpec((tm, d), lambda i: (i, 0)), _resident(g.shape), _resident(w.shape)],
        out_specs=[pl.BlockSpec((tm, HY_MAIN), lambda i: (i, 0)), pl.BlockSpec((tm, HY_SMALL), lambda i: (i, 0))],
        out_shape=[jax.ShapeDtypeStruct((t, HY_MAIN), BF16), jax.ShapeDtypeStruct((t, HY_SMALL), F32)],
        compiler_params=_params("parallel"),
        name="hy_in_proj",
    )(x, g, w)


def _gla_body(q_ref, k_ref, v_ref, r_ref, sm_ref, wg2_ref, bg_ref, ng_ref, o_ref,
              st_scr, qe_scr, qm_scr, km_scr, kd_scr, el_scr, *, n_chunks):
    @pl.when(pl.program_id(1) == 0)
    def _():
        st_scr[...] = jnp.zeros_like(st_scr)

    c = GLA_CHUNK
    row = lax.broadcasted_iota(I32, (c, c), 0)
    col = lax.broadcasted_iota(I32, (c, c), 1)
    tril = row >= col
    tri_f = tril.astype(F32)

    gl = _nn(sm_ref[0].astype(BF16), wg2_ref[...]) + bg_ref[...]
    g = -_softplus(-gl) * (1.0 / GLA_GATE_TAU)
    q = q_ref[0].astype(F32) * (GLA_DK ** -0.5)
    k = k_ref[0].astype(F32)
    for ci in range(n_chunks):
        rs = slice(ci * c, (ci + 1) * c)
        b = jnp.dot(tri_f, g[rs], preferred_element_type=F32, precision=lax.Precision.HIGHEST)
        b_last = b[c - 1:c, :]
        b_mid = b[c // 2 - 1:c // 2, :]
        qe_scr[rs, :] = (q[rs] * jnp.exp(b)).astype(BF16)
        qm_scr[rs, :] = (q[rs] * jnp.exp(b - b_mid)).astype(BF16)
        km_scr[rs, :] = (k[rs] * jnp.exp(b_mid - b)).astype(BF16)
        kd_scr[rs, :] = (k[rs] * jnp.exp(b_last - b)).astype(BF16)
        el_scr[ci] = jnp.broadcast_to(jnp.exp(b_last), (SUBLANES, GLA_QK))

    def chunk(ci, carry):
        rows = pl.ds(pl.multiple_of(ci * c, c), c)
        e_last = el_scr[ci][0:1]
        for h in range(GLA_HEADS):
            ks = slice(h * GLA_DK, (h + 1) * GLA_DK)
            vs = slice(h * GLA_DV, (h + 1) * GLA_DV)
            v = v_ref[0, rows, vs]
            st = st_scr[h]
            attn = jnp.where(tril, _nt(qm_scr[rows, ks], km_scr[rows, ks]), 0.0).astype(BF16)
            o = _nt(qe_scr[rows, ks], st.astype(BF16)) + _nn(attn, v)
            st_scr[h] = st * e_last[:, ks] + _tn(v, kd_scr[rows, ks])
            o = _rms(o, ng_ref[:, vs])
            o_ref[0, rows, vs] = (o * _silu(r_ref[0, rows, vs].astype(F32))).astype(BF16)
        return carry

    lax.fori_loop(0, n_chunks, chunk, 0, unroll=2)


def _gla(p, sm, wg2, bg, ng, *, cs=512):
    bsz, s, _ = p.shape
    cs = min(cs, s)
    return pl.pallas_call(
        functools.partial(_gla_body, n_chunks=cs // GLA_CHUNK),
        grid=(bsz, s // cs),
        in_specs=[
            pl.BlockSpec((1, cs, GLA_QK), lambda b, i: (b, i, 0)),
            pl.BlockSpec((1, cs, GLA_QK), lambda b, i: (b, i, 1)),
            pl.BlockSpec((1, cs, GLA_V), lambda b, i: (b, i, 1)),
            pl.BlockSpec((1, cs, GLA_V), lambda b, i: (b, i, 2)),
            pl.BlockSpec((1, cs, HY_SMALL), lambda b, i: (b, i, 0)),
            _resident(wg2.shape), _resident(bg.shape), _resident(ng.shape),
        ],
        out_specs=pl.BlockSpec((1, cs, GLA_V), lambda b, i: (b, i, 0)),
        out_shape=jax.ShapeDtypeStruct((bsz, s, GLA_V), BF16),
        scratch_shapes=[pltpu.VMEM((GLA_HEADS, GLA_DV, GLA_DK), F32)]
        + [pltpu.VMEM((cs, GLA_QK), BF16)] * 4
        + [pltpu.VMEM((cs // GLA_CHUNK, SUBLANES, GLA_QK), F32)],
        compiler_params=_params("parallel", "arbitrary"),
        name="gla",
    )(p, p, p, p, sm, wg2, bg, ng)


def _ssd_body(z_ref, xs_ref, bc_ref, sm_ref, cwx_ref, cbx_ref, cwb_ref, cbb_ref, dtb_ref, a_ref, d_ref,
              ng_ref, exb_ref, shift_ref, o_ref, xpad, bpad, st_scr, y_scr, *, cs, n_chunks):
    first = pl.program_id(1) == 0
    c = SSD_CHUNK
    halo = c

    @pl.when(first)
    def _():
        st_scr[...] = jnp.zeros_like(st_scr)
        xpad[0:halo, :] = jnp.zeros((halo, SSD_INNER), BF16)
        bpad[0:halo, :] = jnp.zeros((halo, 2 * SSD_BC), BF16)

    @pl.when(jnp.logical_not(first))
    def _():
        xpad[0:halo, :] = xpad[cs:cs + halo, :]
        bpad[0:halo, :] = bpad[cs:cs + halo, :]

    xpad[halo:halo + cs, :] = xs_ref[0]
    bpad[halo:halo + cs, :] = bc_ref[0]

    row = lax.broadcasted_iota(I32, (c, c), 0)
    col = lax.broadcasted_iota(I32, (c, c), 1)
    tril = row >= col
    tri_f = tril.astype(F32)
    lo_half = lax.broadcasted_iota(I32, (c, 2 * SSD_HEADDIM), 1) < SSD_HEADDIM
    gw = SSD_INNER // SSD_GROUPS

    def conv(pad_ref, w_ref, b_ref, start):
        win = pad_ref[pl.ds(start, 2 * c), :]
        delayed = _nn(shift_ref[...], win)
        acc = b_ref[...] + w_ref[SSD_CONV - 1:SSD_CONV, :] * win[c:2 * c, :].astype(F32)
        for kk in range(SSD_CONV - 1):
            acc = acc + w_ref[kk:kk + 1, :] * delayed[kk * c:(kk + 1) * c, :]
        return _silu(acc)

    def spread_exact(v):
        hi = v.astype(BF16)
        rest = v - hi.astype(F32)
        mid = rest.astype(BF16)
        lo = (rest - mid.astype(F32)).astype(BF16)
        pieces = jnp.concatenate([jnp.broadcast_to(p, (BF16_SUBLANES, HY_SMALL)) for p in (hi, mid, lo)], axis=0)
        out = _nn(pieces, exb_ref[...])
        return out[0:1] + out[BF16_SUBLANES:BF16_SUBLANES + 1] + out[2 * BF16_SUBLANES:2 * BF16_SUBLANES + 1]

    def chunk(ci, carry):
        start = pl.multiple_of(ci * c, c)
        rows = pl.ds(start, c)
        xc = conv(xpad, cwx_ref, cbx_ref, start)
        bcc = conv(bpad, cwb_ref, cbb_ref, start)
        dt = _softplus(sm_ref[0, rows, :] + dtb_ref[...])
        da = dt * a_ref[...]
        acum = jnp.dot(tri_f, da, preferred_element_type=F32, precision=lax.Precision.HIGHEST)
        acum_t = acum.T
        a_last = acum[c - 1:c, :]
        fac = jnp.concatenate([dt, jnp.exp(acum), jnp.exp(a_last - acum)], axis=0).astype(BF16)
        fac_x = _nn(fac, exb_ref[...])
        dt_x, ea_x, w_x = fac_x[0:c], fac_x[c:2 * c], fac_x[2 * c:3 * c]
        dec_x = spread_exact(jnp.exp(a_last))
        xdt = xc * dt_x
        xdt_b = xdt.astype(BF16)
        xdtw_b = (xdt * w_x).astype(BF16)
        for grp in range(SSD_GROUPS):
            gs = slice(grp * gw, (grp + 1) * gw)
            bm = bcc[:, grp * SSD_STATE:(grp + 1) * SSD_STATE].astype(BF16)
            cm = bcc[:, SSD_BC + grp * SSD_STATE:SSD_BC + (grp + 1) * SSD_STATE].astype(BF16)
            scores = _nt(cm, bm)
            st = st_scr[:, gs]
            y_inter = _nn(cm, st.astype(BF16)) * ea_x[:, gs]
            st_scr[:, gs] = st * dec_x[:, gs] + _tn(bm, xdtw_b[:, gs])
            for pair in range(SSD_HEADS_PER_GROUP // 2):
                h0 = grp * SSD_HEADS_PER_GROUP + 2 * pair
                ps = slice(h0 * SSD_HEADDIM, (h0 + 2) * SSD_HEADDIM)
                lhs = []
                for h in (h0, h0 + 1):
                    ln = HY_DT_LANE + h
                    seg = jnp.where(tril, jnp.exp(acum[:, ln:ln + 1] - acum_t[ln:ln + 1, :]), 0.0)
                    lhs.append((scores * seg).astype(BF16))
                slab = xdt_b[:, ps]
                zero = jnp.zeros_like(slab)
                rhs = jnp.concatenate([jnp.where(lo_half, slab, zero), jnp.where(lo_half, zero, slab)], axis=0)
                y_pair = _nn(jnp.concatenate(lhs, axis=1), rhs)
                y_scr[:, ps] = (y_pair + y_inter[:, 2 * pair * SSD_HEADDIM:(2 * pair + 2) * SSD_HEADDIM]
                                + d_ref[:, ps] * xc[:, ps])
        y = y_scr[...] * _silu(z_ref[0, rows, :].astype(F32))
        for grp in range(SSD_GROUPS):
            gs = slice(grp * gw, (grp + 1) * gw)
            o_ref[0, rows, gs] = _rms(y[:, gs], ng_ref[:, gs]).astype(BF16)
        return carry

    lax.fori_loop(0, n_chunks, chunk, 0, unroll=2)


def _head_expander():
    e = np.zeros((HY_SMALL, SSD_INNER), np.float32)
    for h in range(SSD_HEADS):
        e[HY_DT_LANE + h, h * SSD_HEADDIM:(h + 1) * SSD_HEADDIM] = 1.0
    return jnp.asarray(e, BF16)


def _conv_delays():
    c = SSD_CHUNK
    m = np.zeros(((SSD_CONV - 1) * c, 2 * c), np.float32)
    for k in range(SSD_CONV - 1):
        for t in range(c):
            m[k * c + t, c + t - (SSD_CONV - 1) + k] = 1.0
    return jnp.asarray(m, BF16)


def _ssd(p, sm, cwx, cbx, cwb, cbb, dtb, a_pad, d_full, ng, *, cs=512):
    bsz, s, _ = p.shape
    cs = min(cs, s)
    consts = (cwx, cbx, cwb, cbb, dtb, a_pad, d_full, ng, _head_expander(), _conv_delays())
    return pl.pallas_call(
        functools.partial(_ssd_body, cs=cs, n_chunks=cs // SSD_CHUNK),
        grid=(bsz, s // cs),
        in_specs=[
            pl.BlockSpec((1, cs, SSD_INNER), lambda b, i: (b, i, 3)),
            pl.BlockSpec((1, cs, SSD_INNER), lambda b, i: (b, i, 4)),
            pl.BlockSpec((1, cs, 2 * SSD_BC), lambda b, i: (b, i, 10)),
            pl.BlockSpec((1, cs, HY_SMALL), lambda b, i: (b, i, 0)),
        ] + [_resident(a.shape) for a in consts],
        out_specs=pl.BlockSpec((1, cs, SSD_INNER), lambda b, i: (b, i, 0)),
        out_shape=jax.ShapeDtypeStruct((bsz, s, SSD_INNER), BF16),
        scratch_shapes=[
            pltpu.VMEM((cs + SSD_CHUNK, SSD_INNER), BF16),
            pltpu.VMEM((cs + SSD_CHUNK, 2 * SSD_BC), BF16),
            pltpu.VMEM((SSD_STATE, SSD_INNER), F32),
            pltpu.VMEM((SSD_CHUNK, SSD_INNER), F32),
        ],
        compiler_params=_params("parallel", "arbitrary"),
        name="ssd",
    )(p, p, p, sm, *consts)


DSA_H_WIDTH = 512
DSA_LAT_EXT = DSA_LATENT + 16


def _dsa_proj_body(x_ref, g_ref, wh_ref, wq_ref, qg_ref, kvg_ref, lng_ref, lnb_ref,
                   q_ref, qi_ref, kv_ref, kvt_ref, ki_ref, wt_ref):
    h = _rms(x_ref[...], g_ref[...]).astype(BF16)
    c = _nn(h, wh_ref[...])
    q_lat = _rms(c[:, :DSA_Q_RANK], qg_ref[...]).astype(BF16)
    qq = _nn(q_lat, wq_ref[...])
    nq = DSA_HEADS * DSA_LATENT
    q_ref[...] = (qq[:, :nq] * (DSA_LATENT ** -0.5 * LOG2E)).astype(BF16)
    qi_ref[...] = qq[:, nq:].astype(BF16)
    kv = _rms(c[:, DSA_Q_RANK:DSA_Q_RANK + DSA_LATENT], kvg_ref[...])
    kv_ref[...] = kv.astype(BF16)
    kvt_ref[0:DSA_LATENT, :] = kv.T.astype(BF16)
    ones_row = lax.broadcasted_iota(I32, (DSA_LAT_EXT - DSA_LATENT, kv.shape[0]), 0) == 0
    kvt_ref[DSA_LATENT:, :] = ones_row.astype(F32).astype(BF16)
    tail = c[:, DSA_Q_RANK + DSA_LATENT:]
    kr = tail[:, :IDX_DIM]
    mu = jnp.mean(kr, axis=-1, keepdims=True)
    var = jnp.mean(jnp.square(kr - mu), axis=-1, keepdims=True)
    ki = (kr - mu) * lax.rsqrt(var + EPS) * lng_ref[...] + lnb_ref[...]
    ki_ref[...] = ki.astype(BF16)
    tail_t = (tail * (IDX_HEADS ** -0.5 * IDX_DIM ** -0.5)).T
    wt_ref[...] = tail_t[IDX_DIM:IDX_DIM + IDX_HEADS, :]


def _dsa_proj(x, g, wh, wq, qg, kvg, lng, lnb, *, tm=512):
    t, d = x.shape
    tm = min(tm, t)
    nq = DSA_HEADS * DSA_LATENT
    ni = IDX_HEADS * IDX_DIM
    full = lambda a: _resident(a.shape)
    tok = lambda w: pl.BlockSpec((tm, w), lambda i: (i, 0))
    return pl.pallas_call(
        _dsa_proj_body,
        grid=(t // tm,),
        in_specs=[tok(d), full(g), full(wh), full(wq), full(qg), full(kvg), full(lng), full(lnb)],
        out_specs=[tok(nq), tok(ni), tok(DSA_LATENT),
                   pl.BlockSpec((DSA_LAT_EXT, tm), lambda i: (0, i)),
                   tok(IDX_DIM),
                   pl.BlockSpec((IDX_HEADS, tm), lambda i: (0, i))],
        out_shape=[
            jax.ShapeDtypeStruct((t, nq), BF16),
            jax.ShapeDtypeStruct((t, ni), BF16),
            jax.ShapeDtypeStruct((t, DSA_LATENT), BF16),
            jax.ShapeDtypeStruct((DSA_LAT_EXT, t), BF16),
            jax.ShapeDtypeStruct((t, IDX_DIM), BF16),
            jax.ShapeDtypeStruct((IDX_HEADS, t), F32),
        ],
        compiler_params=_params("parallel"),
        name="dsa_proj",
    )(x, g, wh, wq, qg, kvg, lng, lnb)


DSA_TQ = 256
PART_ROWS = 4 * SUBLANES


def _dsa_attn_body(q_ref, qi_ref, wt_ref, ki_ref, kv_ref, kvt_ref, bias_ref, x_ref, wuv_ref, wo_ref,
                   o_ref, key_scr, hi_scr, lo_scr, sel_scr, s_scr, acc_scr, m_scr, bm_scr, al_scr, mb_scr, u_scr,
                   *, topk):
    tq = DSA_TQ
    tk = DSA_TQ
    qb = pl.program_id(1)
    krow = lax.broadcasted_iota(I32, (tk, tq), 0)
    qcol = lax.broadcasted_iota(I32, (tk, tq), 1)
    causal = krow <= qcol

    def score_block(row0, n_rows, diag):
        rows = pl.ds(pl.multiple_of(row0, tk), n_rows)
        kk = ki_ref[rows, :]
        s = jnp.zeros((n_rows, tq), F32)
        for h in range(IDX_HEADS):
            z = _nt(kk, qi_ref[:, h * IDX_DIM:(h + 1) * IDX_DIM])
            s = s + jnp.maximum(z, 0.0) * wt_ref[h:h + 1, :]
        if diag:
            s = jnp.where(causal, s, -jnp.inf)
        bits = pltpu.bitcast(s, I32)
        key = bits ^ ((bits >> 31) & 0x7FFFFFFF)
        key_scr[rows, :] = key
        hi_scr[rows, :] = (key >> 16).astype(I16)
        lo_scr[rows, :] = ((key & 0xFFFF) + I16_MIN).astype(I16)

    def score_loop(kp, carry):
        score_block(kp * (2 * tk), 2 * tk, False)
        return carry

    lax.fori_loop(0, qb // 2, score_loop, 0)

    @pl.when(qb % 2 == 1)
    def _():
        score_block((qb - 1) * tk, tk, False)

    score_block(qb * tk, tk, True)
    pad_rows = pl.ds(pl.multiple_of((qb + 1) * tk, tk), tk)
    key_scr[pad_rows, :] = jnp.full((tk, tq), INT_MIN, I32)
    hi_scr[pad_rows, :] = jnp.full((tk, tq), I16_MIN, I16)
    lo_scr[pad_rows, :] = jnp.full((tk, tq), I16_MIN, I16)
    n_pairs = (qb + 2) // 2

    pack_rows = 4 * BF16_SUBLANES

    def pair_rows(kp):
        return pl.ds(pl.multiple_of(kp * (2 * tk), 2 * tk), 2 * tk)

    def select_half(ref):
        def step(it, best):
            cand = best | (jnp.int32(1) << (15 - it))
            cand_s = (cand + I16_MIN).astype(I16)

            def count(kp, cnt):
                one = jnp.where(ref[pair_rows(kp), :] >= cand_s, jnp.ones((), I16), jnp.zeros((), I16))
                for i in range(2 * tk // pack_rows):
                    cnt = cnt + one[i * pack_rows:(i + 1) * pack_rows]
                return cnt

            cnt = lax.fori_loop(0, n_pairs, count, jnp.zeros((pack_rows, tq), I16))
            tot = jnp.sum(cnt.astype(I32), axis=0, keepdims=True)
            return jnp.where(tot >= topk, cand, best)

        return lax.fori_loop(0, 16, step, jnp.zeros((1, tq), I32))

    upper = select_half(hi_scr)
    upper_s = (upper + I16_MIN).astype(I16)

    def restrict(kp, carry):
        hi = hi_scr[pair_rows(kp), :]
        v = jnp.where(hi == upper_s, lo_scr[pair_rows(kp), :], jnp.full((), I16_MIN, I16))
        sel_scr[pair_rows(kp), :] = jnp.where(hi > upper_s, jnp.full((), I16_MAX, I16), v)
        return carry

    lax.fori_loop(0, n_pairs, restrict, 0)
    lower = select_half(sel_scr)
    thr = (((upper + I16_MIN) << 16) | lower)

    m_scr[...] = jnp.full(m_scr.shape, NEG_BIG, F32)
    acc_scr[...] = jnp.zeros_like(acc_scr)
    key_minus_query = krow - qcol

    def block_rows(kb):
        return pl.ds(pl.multiple_of(kb * tk, tk), tk)

    def tiles(x):
        return x.reshape(x.shape[0] // SUBLANES, SUBLANES, tq)

    def block_max(s):
        part = jnp.max(s.reshape(tk // PART_ROWS, PART_ROWS, tq), axis=0)
        m8 = jnp.max(tiles(part), axis=0)
        for shift in (4, 2, 1):
            m8 = jnp.maximum(m8, pltpu.roll(m8, shift, axis=0))
        return m8

    def attn_step(kb, cur, with_logits, with_probs):
        prev = 1 - cur
        if with_logits:
            kvb = kv_ref[block_rows(kb), :]
            sel = jnp.logical_and(key_scr[block_rows(kb), :] >= thr, key_minus_query <= (qb - kb) * tk)
            cap = jnp.where(sel, jnp.inf, NEG_BIG)
        if with_probs:
            kvtb = kvt_ref[:, block_rows(kb - 1)]
        for h in range(DSA_HEADS):
            if with_logits:
                s = _nt(kvb, q_ref[:, h * DSA_LATENT:(h + 1) * DSA_LATENT])
                s = jnp.minimum(s, cap)
                s_scr[cur, h] = tiles(s)
                bm_scr[h] = block_max(s)
            if with_probs:
                p = jnp.exp2(s_scr[prev, h] - mb_scr[prev, h][None]).reshape(tk, tq)
                acc_scr[h] = acc_scr[h] * al_scr[prev, h][None] + tiles(_nn(kvtb, p.astype(BF16)))
        if with_logits:
            @pl.when(kb >= qb - 1)
            def _():
                for h in range(DSA_HEADS):
                    s = s_scr[cur, h].reshape(tk, tq) + bias_ref[h, qb - kb]
                    s_scr[cur, h] = tiles(s)
                    bm_scr[h] = block_max(s)

            m_old = m_scr[...]
            m_new = jnp.maximum(m_old, bm_scr[...])
            al_scr[cur] = jnp.exp2(m_old - m_new)
            mb_scr[cur] = m_new
            m_scr[...] = m_new

    attn_step(0, 0, True, False)

    def attn_pair(j, carry):
        attn_step(2 * j + 1, 1, True, True)
        attn_step(2 * j + 2, 0, True, True)
        return carry

    lax.fori_loop(0, qb // 2, attn_pair, 0)

    @pl.when(qb % 2 == 1)
    def _():
        attn_step(qb, 1, True, True)
        attn_step(qb + 1, 0, False, True)

    @pl.when(qb % 2 == 0)
    def _():
        attn_step(qb + 1, 1, False, True)

    for h in range(DSA_HEADS):
        acc = acc_scr[h].reshape(DSA_LAT_EXT, tq)
        o_t = (acc[0:DSA_LATENT] / acc[DSA_LATENT:DSA_LATENT + 1]).astype(BF16)
        u_scr[h * DSA_VDIM:(h + 1) * DSA_VDIM, :] = _nn(wuv_ref[h], o_t).astype(BF16)
    y_t = _nn(wo_ref[...], u_scr[...])
    o_ref[...] = x_ref[...] + y_t.T


def _dsa_attn(x, q, qi, wt, ki, kv, kvt, bias_t, wuv_t, wo_t, *, bsz, s):
    t, d = x.shape
    tq = DSA_TQ
    nq = s // tq
    topk = min(TOPK_MAX, s // 4)
    full = lambda a: _resident(a.shape)
    return pl.pallas_call(
        functools.partial(_dsa_attn_body, topk=topk),
        grid=(bsz, nq),
        in_specs=[
            pl.BlockSpec((tq, q.shape[1]), lambda b, i: (b * nq + i, 0)),
            pl.BlockSpec((tq, qi.shape[1]), lambda b, i: (b * nq + i, 0)),
            pl.BlockSpec((IDX_HEADS, tq), lambda b, i: (0, b * nq + i)),
            pl.BlockSpec((s, IDX_DIM), lambda b, i: (b, 0)),
            pl.BlockSpec((s, DSA_LATENT), lambda b, i: (b, 0)),
            pl.BlockSpec((DSA_LAT_EXT, s), lambda b, i: (0, b)),
            full(bias_t),
            pl.BlockSpec((tq, d), lambda b, i: (b * nq + i, 0)),
            full(wuv_t), full(wo_t),
        ],
        out_specs=pl.BlockSpec((tq, d), lambda b, i: (b * nq + i, 0)),
        out_shape=jax.ShapeDtypeStruct((t, d), F32),
        scratch_shapes=[
            pltpu.VMEM((s + tq, tq), I32),
            pltpu.VMEM((s + tq, tq), I16),
            pltpu.VMEM((s + tq, tq), I16),
            pltpu.VMEM((s + tq, tq), I16),
            pltpu.VMEM((2, DSA_HEADS, tq // SUBLANES, SUBLANES, tq), F32),
            pltpu.VMEM((DSA_HEADS, DSA_LAT_EXT // SUBLANES, SUBLANES, tq), F32),
            pltpu.VMEM((DSA_HEADS, SUBLANES, tq), F32),
            pltpu.VMEM((DSA_HEADS, SUBLANES, tq), F32),
            pltpu.VMEM((2, DSA_HEADS, SUBLANES, tq), F32),
            pltpu.VMEM((2, DSA_HEADS, SUBLANES, tq), F32),
            pltpu.VMEM((DSA_HEADS * DSA_VDIM, tq), BF16),
        ],
        compiler_params=_params("parallel", "arbitrary"),
        name="dsa_attn",
    )(q, qi, wt, ki, kv, kvt, bias_t, x, wuv_t, wo_t)


def _t5_bucket(rel):
    max_exact = REL_BUCKETS // 2
    relf = jnp.maximum(rel, 1).astype(F32)
    large = max_exact + (jnp.log(relf / max_exact) / math.log(REL_MAX_DIST / max_exact)
                         * (REL_BUCKETS - max_exact)).astype(I32)
    large = jnp.minimum(large, REL_BUCKETS - 1)
    return jnp.where(rel < max_exact, rel, large)


def _bias_tiles(rel_bias):
    tq = DSA_TQ
    kk = jnp.arange(tq)[:, None]
    qq = jnp.arange(tq)[None, :]
    rel = jnp.stack([qq - kk, qq - kk + tq])
    table = (rel_bias - rel_bias[REL_BUCKETS - 1][None, :]) * LOG2E
    onehot = (_t5_bucket(jnp.maximum(rel, 0))[..., None] == jnp.arange(REL_BUCKETS)).astype(F32)
    return jnp.einsum("kabn,nh->hkab", onehot, table, precision=lax.Precision.HIGHEST)


def _hy_in_weight(w_in):
    sizes = (GLA_QK, GLA_QK, GLA_V, GLA_V, GLA_GATE_RANK, SSD_INNER, SSD_INNER + 2 * SSD_BC, SSD_HEADS)
    offs = np.cumsum((0,) + sizes)
    q, k, v, r, g_lr, z, xbc, dt = [w_in[:, offs[i]:offs[i + 1]] for i in range(len(sizes))]
    pad = jnp.zeros((w_in.shape[0], HY_SMALL - GLA_GATE_RANK - SSD_HEADS), w_in.dtype)
    return jnp.concatenate([q, k, v, r, z, xbc, g_lr, dt, pad], axis=1).astype(BF16)


def kernel(x, ffn_norm_g, ffn_w_gate, ffn_w_up, ffn_w_down, mix_norm_g, hy_w_in, gla_w_gate2, gla_b_gate, gla_norm_g, ssd_conv_w, ssd_conv_b, ssd_dt_bias, ssd_a_log, ssd_d, ssd_norm_g, hy_w_out, dsa_w_dq, dsa_q_norm_g, dsa_w_uq, dsa_w_dkv, dsa_kv_norm_g, dsa_w_uv, dsa_w_o, idx_w_q, idx_w_k, idx_ln_g, idx_ln_b, idx_w_w, rel_bias, final_norm_g):
    bsz, s, d = x.shape
    t = bsz * s
    xt = x.reshape(t, d)
    row = lambda a: a.reshape(1, -1).astype(F32)

    wg_all, wu_all, wd_all = (w.astype(BF16) for w in (ffn_w_gate, ffn_w_up, ffn_w_down))

    def ffn(xt, layer, half, final, mix=None):
        return _ffn(xt, row(ffn_norm_g[layer, half]), wg_all, wu_all, wd_all, row(final_norm_g),
                    layer=layer, half=half, final_norm=final, mix=mix)

    xt = ffn(xt, 0, 0, False)
    p, sm = _in_proj(xt, row(mix_norm_g[0]), _hy_in_weight(hy_w_in[0]))
    p = p.reshape(bsz, s, HY_MAIN)
    sm = sm.reshape(bsz, s, HY_SMALL)
    wg2 = jnp.zeros((HY_SMALL, GLA_QK), F32).at[:GLA_GATE_RANK].set(gla_w_gate2[0]).astype(BF16)
    o_gla = _gla(p, sm, wg2, row(gla_b_gate[0]), row(gla_norm_g[0]))
    cw, cb = ssd_conv_w[0].astype(F32), ssd_conv_b[0].astype(F32)
    lane_pad = lambda v: jnp.zeros((1, HY_SMALL), F32).at[0, HY_DT_LANE:HY_DT_LANE + SSD_HEADS].set(v)
    y_ssd = _ssd(p, sm, cw[:, :SSD_INNER], row(cb[:SSD_INNER]), cw[:, SSD_INNER:], row(cb[SSD_INNER:]),
                 lane_pad(ssd_dt_bias[0].astype(F32)), lane_pad(-jnp.exp(ssd_a_log[0].astype(F32))),
                 row(jnp.repeat(ssd_d[0], SSD_HEADDIM)), row(ssd_norm_g[0]))
    w_out = hy_w_out[0].astype(BF16)
    xt = ffn(xt, 0, 1, False,
             mix=(o_gla.reshape(t, GLA_V), y_ssd.reshape(t, SSD_INNER), w_out[:GLA_V], w_out[GLA_V:]))

    xt = ffn(xt, 1, 0, False)
    wh = jnp.concatenate(
        [dsa_w_dq[0], dsa_w_dkv[0], idx_w_k[0], idx_w_w[0],
         jnp.zeros((d, DSA_H_WIDTH - DSA_Q_RANK - DSA_LATENT - IDX_DIM - IDX_HEADS), F32)], axis=1).astype(BF16)
    wq = jnp.concatenate([dsa_w_uq[0], idx_w_q[0]], axis=1).astype(BF16)
    q, qi, kv, kvt, ki, wt = _dsa_proj(xt, row(mix_norm_g[1]), wh, wq, row(dsa_q_norm_g[0]),
                                       row(dsa_kv_norm_g[0]), row(idx_ln_g[0]), row(idx_ln_b[0]))
    wuv_t = jnp.transpose(dsa_w_uv[0], (0, 2, 1)).astype(BF16)
    wo_t = dsa_w_o[0].T.astype(BF16)
    xt = _dsa_attn(xt, q, qi, wt, ki, kv, kvt, _bias_tiles(rel_bias.astype(F32)), wuv_t, wo_t, bsz=bsz, s=s)
    xt = ffn(xt, 1, 1, True)
    return xt.reshape(bsz, s, d)
```

```python
import functools
import math

import numpy as np
import jax
import jax.numpy as jnp
from jax import lax
from jax.experimental import pallas as pl
from jax.experimental.pallas import tpu as pltpu

F32 = jnp.float32
BF16 = jnp.bfloat16
I32 = jnp.int32
I16 = jnp.int16
I16_MIN = -2 ** 15
I16_MAX = 2 ** 15 - 1

EPS = 1e-6
D_MODEL = 1024
D_FF = 2816
GLA_HEADS = 4
GLA_DK = 128
GLA_DV = 256
GLA_GATE_RANK = 16
GLA_GATE_TAU = 16.0
GLA_CHUNK = 128
GLA_QK = GLA_HEADS * GLA_DK
GLA_V = GLA_HEADS * GLA_DV
SSD_HEADS = 16
SSD_HEADDIM = 64
SSD_STATE = 128
SSD_GROUPS = 2
SSD_CONV = 4
SSD_CHUNK = 128
SSD_INNER = SSD_HEADS * SSD_HEADDIM
SSD_BC = SSD_GROUPS * SSD_STATE
SSD_HEADS_PER_GROUP = SSD_HEADS // SSD_GROUPS
DSA_HEADS = 16
DSA_Q_RANK = 256
DSA_LATENT = 128
DSA_VDIM = 64
IDX_HEADS = 16
IDX_DIM = 64
TOPK_MAX = 256
REL_BUCKETS = 32
REL_MAX_DIST = 128

LANES = 128
SUBLANES = 8
BF16_SUBLANES = 16
VMEM_LIMIT = 56 * 1024 * 1024

HY_SMALL = LANES
HY_MAIN = 2 * GLA_QK + 2 * GLA_V + SSD_INNER + SSD_INNER + 2 * SSD_BC
HY_WIDTH = HY_MAIN + HY_SMALL
HY_DT_LANE = GLA_GATE_RANK

NEG_BIG = -1e30
LOG2E = math.log2(math.e)
INT_MIN = -2 ** 31


def _nt(a, b):
    return lax.dot_general(a, b, (((1,), (1,)), ((), ())), preferred_element_type=F32)


def _tn(a, b):
    return lax.dot_general(a, b, (((0,), (0,)), ((), ())), preferred_element_type=F32)


def _nn(a, b):
    return jnp.dot(a, b, preferred_element_type=F32)


def _silu(a):
    return a / (1.0 + jnp.exp(-a))


def _softplus(a):
    return jnp.maximum(a, 0.0) + jnp.log1p(jnp.exp(-jnp.abs(a)))


def _rms(x, g):
    return x * lax.rsqrt(jnp.mean(x * x, axis=-1, keepdims=True) + EPS) * g


def _params(*sem):
    return pltpu.CompilerParams(dimension_semantics=sem, vmem_limit_bytes=VMEM_LIMIT)


def _ffn_body(x_ref, g_ref, wg_ref, wu_ref, wd_ref, fg_ref, *rest, final_norm):
    o_ref = rest[-1]
    x = x_ref[...]
    if len(rest) > 1:
        a_ref, b_ref, wa_ref, wb_ref = rest[:-1]
        x = x + _nn(a_ref[...], wa_ref[...]) + _nn(b_ref[...], wb_ref[...])
    h = _rms(x, g_ref[...]).astype(BF16)
    a = _nn(h, wg_ref[...])
    b = _nn(h, wu_ref[...])
    t = (_silu(a) * b).astype(BF16)
    y = x + 0.5 * _nn(t, wd_ref[...])
    if final_norm:
        y = _rms(y, fg_ref[...])
    o_ref[...] = y


def _resident(shape):
    return pl.BlockSpec(shape, lambda *_: (0,) * len(shape), pipeline_mode=pl.Buffered(1))


def _ffn(x, g, wg, wu, wd, fg, *, layer, half, final_norm, mix=None, tm=512):
    t, d = x.shape
    tm = min(tm, t)
    tok = lambda w: pl.BlockSpec((tm, w), lambda i: (i, 0))
    pick = lambda w: pl.BlockSpec((None, None) + w.shape[2:], lambda i: (layer, half, 0, 0),
                                  pipeline_mode=pl.Buffered(1))
    mix_specs, mix_args = [], ()
    if mix is not None:
        a, b, wa, wb = mix_args = mix
        mix_specs = [tok(a.shape[1]), tok(b.shape[1]), _resident(wa.shape), _resident(wb.shape)]
    return pl.pallas_call(
        functools.partial(_ffn_body, final_norm=final_norm),
        grid=(t // tm,),
        in_specs=[tok(d), _resident(g.shape), pick(wg), pick(wu), pick(wd), _resident(fg.shape)] + mix_specs,
        out_specs=tok(d),
        out_shape=jax.ShapeDtypeStruct((t, d), F32),
        compiler_params=_params("parallel"),
        name="ffn",
    )(x, g, wg, wu, wd, fg, *mix_args)


def _in_proj_body(x_ref, g_ref, w_ref, o_ref, sm_ref):
    h = _rms(x_ref[...], g_ref[...]).astype(BF16)
    o_ref[...] = _nn(h, w_ref[:, :HY_MAIN]).astype(BF16)
    sm_ref[...] = _nn(h, w_ref[:, HY_MAIN:])


def _in_proj(x, g, w, *, tm=512):
    t, d = x.shape
    tm = min(tm, t)
    return pl.pallas_call(
        _in_proj_body,
        grid=(t // tm,),
        in_specs=[pl.BlockSpec((tm, d), lambda i: (i, 0)), _resident(g.shape), _resident(w.shape)],
        out_specs=[pl.BlockSpec((tm, HY_MAIN), lambda i: (i, 0)), pl.BlockSpec((tm, HY_SMALL), lambda i: (i, 0))],
        out_shape=[jax.ShapeDtypeStruct((t, HY_MAIN), BF16), jax.ShapeDtypeStruct((t, HY_SMALL), F32)],
        compiler_params=_params("parallel"),
        name="hy_in_proj",
    )(x, g, w)


def _gla_body(q_ref, k_ref, v_ref, r_ref, sm_ref, wg2_ref, bg_ref, ng_ref, o_ref,
              st_scr, qe_scr, qm_scr, km_scr, kd_scr, el_scr, *, n_chunks):
    @pl.when(pl.program_id(1) == 0)
    def _():
        st_scr[...] = jnp.zeros_like(st_scr)

    c = GLA_CHUNK
    row = lax.broadcasted_iota(I32, (c, c), 0)
    col = lax.broadcasted_iota(I32, (c, c), 1)
    tril = row >= col
    tri_f = tril.astype(F32)

    gl = _nn(sm_ref[0].astype(BF16), wg2_ref[...]) + bg_ref[...]
    g = -_softplus(-gl) * (1.0 / GLA_GATE_TAU)
    q = q_ref[0].astype(F32) * (GLA_DK ** -0.5)
    k = k_ref[0].astype(F32)
    for ci in range(n_chunks):
        rs = slice(ci * c, (ci + 1) * c)
        b = jnp.dot(tri_f, g[rs], preferred_element_type=F32, precision=lax.Precision.HIGHEST)
        b_last = b[c - 1:c, :]
        b_mid = b[c // 2 - 1:c // 2, :]
        qe_scr[rs, :] = (q[rs] * jnp.exp(b)).astype(BF16)
        qm_scr[rs, :] = (q[rs] * jnp.exp(b - b_mid)).astype(BF16)
        km_scr[rs, :] = (k[rs] * jnp.exp(b_mid - b)).astype(BF16)
        kd_scr[rs, :] = (k[rs] * jnp.exp(b_last - b)).astype(BF16)
        el_scr[ci] = jnp.broadcast_to(jnp.exp(b_last), (SUBLANES, GLA_QK))

    def chunk(ci, carry):
        rows = pl.ds(pl.multiple_of(ci * c, c), c)
        e_last = el_scr[ci][0:1]
        for h in range(GLA_HEADS):
            ks = slice(h * GLA_DK, (h + 1) * GLA_DK)
            vs = slice(h * GLA_DV, (h + 1) * GLA_DV)
            v = v_ref[0, rows, vs]
            st = st_scr[h]
            attn = jnp.where(tril, _nt(qm_scr[rows, ks], km_scr[rows, ks]), 0.0).astype(BF16)
            o = _nt(qe_scr[rows, ks], st.astype(BF16)) + _nn(attn, v)
            st_scr[h] = st * e_last[:, ks] + _tn(v, kd_scr[rows, ks])
            o = _rms(o, ng_ref[:, vs])
            o_ref[0, rows, vs] = (o * _silu(r_ref[0, rows, vs].astype(F32))).astype(BF16)
        return carry

    lax.fori_loop(0, n_chunks, chunk, 0, unroll=2)


def _gla(p, sm, wg2, bg, ng, *, cs=1024):
    bsz, s, _ = p.shape
    cs = min(cs, s)
    return pl.pallas_call(
        functools.partial(_gla_body, n_chunks=cs // GLA_CHUNK),
        grid=(bsz, s // cs),
        in_specs=[
            pl.BlockSpec((1, cs, GLA_QK), lambda b, i: (b, i, 0)),
            pl.BlockSpec((1, cs, GLA_QK), lambda b, i: (b, i, 1)),
            pl.BlockSpec((1, cs, GLA_V), lambda b, i: (b, i, 1)),
            pl.BlockSpec((1, cs, GLA_V), lambda b, i: (b, i, 2)),
            pl.BlockSpec((1, cs, HY_SMALL), lambda b, i: (b, i, 0)),
            _resident(wg2.shape), _resident(bg.shape), _resident(ng.shape),
        ],
        out_specs=pl.BlockSpec((1, cs, GLA_V), lambda b, i: (b, i, 0)),
        out_shape=jax.ShapeDtypeStruct((bsz, s, GLA_V), BF16),
        scratch_shapes=[pltpu.VMEM((GLA_HEADS, GLA_DV, GLA_DK), F32)]
        + [pltpu.VMEM((cs, GLA_QK), BF16)] * 4
        + [pltpu.VMEM((cs // GLA_CHUNK, SUBLANES, GLA_QK), F32)],
        compiler_params=_params("parallel", "arbitrary"),
        name="gla",
    )(p, p, p, p, sm, wg2, bg, ng)


def _ssd_body(z_ref, xs_ref, bc_ref, sm_ref, cwx_ref, cbx_ref, cwb_ref, cbb_ref, dtb_ref, a_ref, d_ref,
              ng_ref, exb_ref, shift_ref, o_ref, xpad, bpad, st_scr, y_scr, *, cs, n_chunks):
    first = pl.program_id(1) == 0
    c = SSD_CHUNK
    halo = c

    @pl.when(first)
    def _():
        st_scr[...] = jnp.zeros_like(st_scr)
        xpad[0:halo, :] = jnp.zeros((halo, SSD_INNER), BF16)
        bpad[0:halo, :] = jnp.zeros((halo, 2 * SSD_BC), BF16)

    @pl.when(jnp.logical_not(first))
    def _():
        xpad[0:halo, :] = xpad[cs:cs + halo, :]
        bpad[0:halo, :] = bpad[cs:cs + halo, :]

    xpad[halo:halo + cs, :] = xs_ref[0]
    bpad[halo:halo + cs, :] = bc_ref[0]

    row = lax.broadcasted_iota(I32, (c, c), 0)
    col = lax.broadcasted_iota(I32, (c, c), 1)
    tril = row >= col
    tri_f = tril.astype(F32)
    lo_half = lax.broadcasted_iota(I32, (c, 2 * SSD_HEADDIM), 1) < SSD_HEADDIM
    gw = SSD_INNER // SSD_GROUPS

    def conv(pad_ref, w_ref, b_ref, start):
        win = pad_ref[pl.ds(start, 2 * c), :]
        delayed = _nn(shift_ref[...], win)
        acc = b_ref[...] + w_ref[SSD_CONV - 1:SSD_CONV, :] * win[c:2 * c, :].astype(F32)
        for kk in range(SSD_CONV - 1):
            acc = acc + w_ref[kk:kk + 1, :] * delayed[kk * c:(kk + 1) * c, :]
        return _silu(acc)

    def spread_exact(v):
        hi = v.astype(BF16)
        rest = v - hi.astype(F32)
        mid = rest.astype(BF16)
        lo = (rest - mid.astype(F32)).astype(BF16)
        pieces = jnp.concatenate([jnp.broadcast_to(p, (BF16_SUBLANES, HY_SMALL)) for p in (hi, mid, lo)], axis=0)
        out = _nn(pieces, exb_ref[...])
        return out[0:1] + out[BF16_SUBLANES:BF16_SUBLANES + 1] + out[2 * BF16_SUBLANES:2 * BF16_SUBLANES + 1]

    def chunk(ci, carry):
        start = pl.multiple_of(ci * c, c)
        rows = pl.ds(start, c)
        xc = conv(xpad, cwx_ref, cbx_ref, start)
        bcc = conv(bpad, cwb_ref, cbb_ref, start)
        dt = _softplus(sm_ref[0, rows, :] + dtb_ref[...])
        da = dt * a_ref[...]
        acum = jnp.dot(tri_f, da, preferred_element_type=F32, precision=lax.Precision.HIGHEST)
        acum_t = acum.T
        a_last = acum[c - 1:c, :]
        fac = jnp.concatenate([dt, jnp.exp(acum), jnp.exp(a_last - acum)], axis=0).astype(BF16)
        fac_x = _nn(fac, exb_ref[...])
        dt_x, ea_x, w_x = fac_x[0:c], fac_x[c:2 * c], fac_x[2 * c:3 * c]
        dec_x = spread_exact(jnp.exp(a_last))
        xdt = xc * dt_x
        xdt_b = xdt.astype(BF16)
        xdtw_b = (xdt * w_x).astype(BF16)
        for grp in range(SSD_GROUPS):
            gs = slice(grp * gw, (grp + 1) * gw)
            bm = bcc[:, grp * SSD_STATE:(grp + 1) * SSD_STATE].astype(BF16)
            cm = bcc[:, SSD_BC + grp * SSD_STATE:SSD_BC + (grp + 1) * SSD_STATE].astype(BF16)
            scores = _nt(cm, bm)
            st = st_scr[:, gs]
            y_inter = _nn(cm, st.astype(BF16)) * ea_x[:, gs]
            st_scr[:, gs] = st * dec_x[:, gs] + _tn(bm, xdtw_b[:, gs])
            for pair in range(SSD_HEADS_PER_GROUP // 2):
                h0 = grp * SSD_HEADS_PER_GROUP + 2 * pair
                ps = slice(h0 * SSD_HEADDIM, (h0 + 2) * SSD_HEADDIM)
                lhs = []
                for h in (h0, h0 + 1):
                    ln = HY_DT_LANE + h
                    seg = jnp.where(tril, jnp.exp(acum[:, ln:ln + 1] - acum_t[ln:ln + 1, :]), 0.0)
                    lhs.append((scores * seg).astype(BF16))
                slab = xdt_b[:, ps]
                zero = jnp.zeros_like(slab)
                rhs = jnp.concatenate([jnp.where(lo_half, slab, zero), jnp.where(lo_half, zero, slab)], axis=0)
                y_pair = _nn(jnp.concatenate(lhs, axis=1), rhs)
                y_scr[:, ps] = (y_pair + y_inter[:, 2 * pair * SSD_HEADDIM:(2 * pair + 2) * SSD_HEADDIM]
                                + d_ref[:, ps] * xc[:, ps])
        y = y_scr[...] * _silu(z_ref[0, rows, :].astype(F32))
        for grp in range(SSD_GROUPS):
            gs = slice(grp * gw, (grp + 1) * gw)
            o_ref[0, rows, gs] = _rms(y[:, gs], ng_ref[:, gs]).astype(BF16)
        return carry

    lax.fori_loop(0, n_chunks, chunk, 0, unroll=2)


def _head_expander():
    e = np.zeros((HY_SMALL, SSD_INNER), np.float32)
    for h in range(SSD_HEADS):
        e[HY_DT_LANE + h, h * SSD_HEADDIM:(h + 1) * SSD_HEADDIM] = 1.0
    return jnp.asarray(e, BF16)


def _conv_delays():
    c = SSD_CHUNK
    m = np.zeros(((SSD_CONV - 1) * c, 2 * c), np.float32)
    for k in range(SSD_CONV - 1):
        for t in range(c):
            m[k * c + t, c + t - (SSD_CONV - 1) + k] = 1.0
    return jnp.asarray(m, BF16)


def _ssd(p, sm, cwx, cbx, cwb, cbb, dtb, a_pad, d_full, ng, *, cs=1024):
    bsz, s, _ = p.shape
    cs = min(cs, s)
    consts = (cwx, cbx, cwb, cbb, dtb, a_pad, d_full, ng, _head_expander(), _conv_delays())
    return pl.pallas_call(
        functools.partial(_ssd_body, cs=cs, n_chunks=cs // SSD_CHUNK),
        grid=(bsz, s // cs),
        in_specs=[
            pl.BlockSpec((1, cs, SSD_INNER), lambda b, i: (b, i, 3)),
            pl.BlockSpec((1, cs, SSD_INNER), lambda b, i: (b, i, 4)),
            pl.BlockSpec((1, cs, 2 * SSD_BC), lambda b, i: (b, i, 10)),
            pl.BlockSpec((1, cs, HY_SMALL), lambda b, i: (b, i, 0)),
        ] + [_resident(a.shape) for a in consts],
        out_specs=pl.BlockSpec((1, cs, SSD_INNER), lambda b, i: (b, i, 0)),
        out_shape=jax.ShapeDtypeStruct((bsz, s, SSD_INNER), BF16),
        scratch_shapes=[
            pltpu.VMEM((cs + SSD_CHUNK, SSD_INNER), BF16),
            pltpu.VMEM((cs + SSD_CHUNK, 2 * SSD_BC), BF16),
            pltpu.VMEM((SSD_STATE, SSD_INNER), F32),
            pltpu.VMEM((SSD_CHUNK, SSD_INNER), F32),
        ],
        compiler_params=_params("parallel", "arbitrary"),
        name="ssd",
    )(p, p, p, sm, *consts)


DSA_H_WIDTH = 512
DSA_LAT_EXT = DSA_LATENT + 16


def _dsa_proj_body(x_ref, g_ref, wh_ref, wq_ref, qg_ref, kvg_ref, lng_ref, lnb_ref,
                   q_ref, qi_ref, kv_ref, kvt_ref, ki_ref, wt_ref):
    h = _rms(x_ref[...], g_ref[...]).astype(BF16)
    c = _nn(h, wh_ref[...])
    q_lat = _rms(c[:, :DSA_Q_RANK], qg_ref[...]).astype(BF16)
    qq = _nn(q_lat, wq_ref[...])
    nq = DSA_HEADS * DSA_LATENT
    q_ref[...] = (qq[:, :nq] * (DSA_LATENT ** -0.5 * LOG2E)).astype(BF16)
    qi_ref[...] = qq[:, nq:].astype(BF16)
    kv = _rms(c[:, DSA_Q_RANK:DSA_Q_RANK + DSA_LATENT], kvg_ref[...])
    kv_ref[...] = kv.astype(BF16)
    kvt_ref[0:DSA_LATENT, :] = kv.T.astype(BF16)
    ones_row = lax.broadcasted_iota(I32, (DSA_LAT_EXT - DSA_LATENT, kv.shape[0]), 0) == 0
    kvt_ref[DSA_LATENT:, :] = ones_row.astype(F32).astype(BF16)
    tail = c[:, DSA_Q_RANK + DSA_LATENT:]
    kr = tail[:, :IDX_DIM]
    mu = jnp.mean(kr, axis=-1, keepdims=True)
    var = jnp.mean(jnp.square(kr - mu), axis=-1, keepdims=True)
    ki = (kr - mu) * lax.rsqrt(var + EPS) * lng_ref[...] + lnb_ref[...]
    ki_ref[...] = ki.astype(BF16)
    tail_t = (tail * (IDX_HEADS ** -0.5 * IDX_DIM ** -0.5)).T
    wt_ref[...] = tail_t[IDX_DIM:IDX_DIM + IDX_HEADS, :]


def _dsa_proj(x, g, wh, wq, qg, kvg, lng, lnb, *, tm=512):
    t, d = x.shape
    tm = min(tm, t)
    nq = DSA_HEADS * DSA_LATENT
    ni = IDX_HEADS * IDX_DIM
    full = lambda a: _resident(a.shape)
    tok = lambda w: pl.BlockSpec((tm, w), lambda i: (i, 0))
    return pl.pallas_call(
        _dsa_proj_body,
        grid=(t // tm,),
        in_specs=[tok(d), full(g), full(wh), full(wq), full(qg), full(kvg), full(lng), full(lnb)],
        out_specs=[tok(nq), tok(ni), tok(DSA_LATENT),
                   pl.BlockSpec((DSA_LAT_EXT, tm), lambda i: (0, i)),
                   tok(IDX_DIM),
                   pl.BlockSpec((IDX_HEADS, tm), lambda i: (0, i))],
        out_shape=[
            jax.ShapeDtypeStruct((t, nq), BF16),
            jax.ShapeDtypeStruct((t, ni), BF16),
            jax.ShapeDtypeStruct((t, DSA_LATENT), BF16),
            jax.ShapeDtypeStruct((DSA_LAT_EXT, t), BF16),
            jax.ShapeDtypeStruct((t, IDX_DIM), BF16),
            jax.ShapeDtypeStruct((IDX_HEADS, t), F32),
        ],
        compiler_params=_params("parallel"),
        name="dsa_proj",
    )(x, g, wh, wq, qg, kvg, lng, lnb)


DSA_TQ = 256
PART_ROWS = 4 * SUBLANES


def _dsa_attn_body(q_ref, qi_ref, wt_ref, ki_ref, kv_ref, kvt_ref, bias_ref, x_ref, wuv_ref, wo_ref,
                   o_ref, key_scr, hi_scr, lo_scr, sel_scr, s_scr, acc_scr, m_scr, bm_scr, al_scr, mb_scr, u_scr,
                   *, topk):
    tq = DSA_TQ
    tk = DSA_TQ
    qb = pl.program_id(1)
    krow = lax.broadcasted_iota(I32, (tk, tq), 0)
    qcol = lax.broadcasted_iota(I32, (tk, tq), 1)
    causal = krow <= qcol

    def score_block(row0, n_rows, diag):
        rows = pl.ds(pl.multiple_of(row0, tk), n_rows)
        kk = ki_ref[rows, :]
        s = jnp.zeros((n_rows, tq), F32)
        for h in range(IDX_HEADS):
            z = _nt(kk, qi_ref[:, h * IDX_DIM:(h + 1) * IDX_DIM])
            s = s + jnp.maximum(z, 0.0) * wt_ref[h:h + 1, :]
        if diag:
            s = jnp.where(causal, s, -jnp.inf)
        bits = pltpu.bitcast(s, I32)
        key = bits ^ ((bits >> 31) & 0x7FFFFFFF)
        key_scr[rows, :] = key
        hi_scr[rows, :] = (key >> 16).astype(I16)
        lo_scr[rows, :] = ((key & 0xFFFF) + I16_MIN).astype(I16)

    def score_loop(kp, carry):
        score_block(kp * (2 * tk), 2 * tk, False)
        return carry

    lax.fori_loop(0, qb // 2, score_loop, 0)

    @pl.when(qb % 2 == 1)
    def _():
        score_block((qb - 1) * tk, tk, False)

    score_block(qb * tk, tk, True)
    pad_rows = pl.ds(pl.multiple_of((qb + 1) * tk, tk), tk)
    key_scr[pad_rows, :] = jnp.full((tk, tq), INT_MIN, I32)
    hi_scr[pad_rows, :] = jnp.full((tk, tq), I16_MIN, I16)
    lo_scr[pad_rows, :] = jnp.full((tk, tq), I16_MIN, I16)
    n_pairs = (qb + 2) // 2

    pack_rows = 4 * BF16_SUBLANES

    def pair_rows(kp):
        return pl.ds(pl.multiple_of(kp * (2 * tk), 2 * tk), 2 * tk)

    def select_half(ref):
        def step(it, best):
            cand = best | (jnp.int32(1) << (15 - it))
            cand_s = (cand + I16_MIN).astype(I16)

            def count(kp, cnt):
                one = jnp.where(ref[pair_rows(kp), :] >= cand_s, jnp.ones((), I16), jnp.zeros((), I16))
                for i in range(2 * tk // pack_rows):
                    cnt = cnt + one[i * pack_rows:(i + 1) * pack_rows]
                return cnt

            cnt = lax.fori_loop(0, n_pairs, count, jnp.zeros((pack_rows, tq), I16))
            tot = jnp.sum(cnt.astype(I32), axis=0, keepdims=True)
            return jnp.where(tot >= topk, cand, best)

        return lax.fori_loop(0, 16, step, jnp.zeros((1, tq), I32))

    upper = select_half(hi_scr)
    upper_s = (upper + I16_MIN).astype(I16)

    def restrict(kp, carry):
        hi = hi_scr[pair_rows(kp), :]
        v = jnp.where(hi == upper_s, lo_scr[pair_rows(kp), :], jnp.full((), I16_MIN, I16))
        sel_scr[pair_rows(kp), :] = jnp.where(hi > upper_s, jnp.full((), I16_MAX, I16), v)
        return carry

    lax.fori_loop(0, n_pairs, restrict, 0)
    lower = select_half(sel_scr)
    thr = (((upper + I16_MIN) << 16) | lower)

    m_scr[...] = jnp.full(m_scr.shape, NEG_BIG, F32)
    acc_scr[...] = jnp.zeros_like(acc_scr)
    key_minus_query = krow - qcol

    def block_rows(kb):
        return pl.ds(pl.multiple_of(kb * tk, tk), tk)

    def tiles(x):
        return x.reshape(x.shape[0] // SUBLANES, SUBLANES, tq)

    def block_max(s):
        part = jnp.max(s.reshape(tk // PART_ROWS, PART_ROWS, tq), axis=0)
        m8 = jnp.max(tiles(part), axis=0)
        for shift in (4, 2, 1):
            m8 = jnp.maximum(m8, pltpu.roll(m8, shift, axis=0))
        return m8

    def attn_step(kb, cur, with_logits, with_probs):
        prev = 1 - cur
        if with_logits:
            kvb = kv_ref[block_rows(kb), :]
            sel = jnp.logical_and(key_scr[block_rows(kb), :] >= thr, key_minus_query <= (qb - kb) * tk)
            cap = jnp.where(sel, jnp.inf, NEG_BIG)
        if with_probs:
            kvtb = kvt_ref[:, block_rows(kb - 1)]
        for h in range(DSA_HEADS):
            if with_logits:
                s = _nt(kvb, q_ref[:, h * DSA_LATENT:(h + 1) * DSA_LATENT])
                s = jnp.minimum(s, cap)
                s_scr[cur, h] = tiles(s)
                bm_scr[h] = block_max(s)
            if with_probs:
                p = jnp.exp2(s_scr[prev, h] - mb_scr[prev, h][None]).reshape(tk, tq)
                acc_scr[h] = acc_scr[h] * al_scr[prev, h][None] + tiles(_nn(kvtb, p.astype(BF16)))
        if with_logits:
            @pl.when(kb >= qb - 1)
            def _():
                for h in range(DSA_HEADS):
                    s = s_scr[cur, h].reshape(tk, tq) + bias_ref[h, qb - kb]
                    s_scr[cur, h] = tiles(s)
                    bm_scr[h] = block_max(s)

            m_old = m_scr[...]
            m_new = jnp.maximum(m_old, bm_scr[...])
            al_scr[cur] = jnp.exp2(m_old - m_new)
            mb_scr[cur] = m_new
            m_scr[...] = m_new

    attn_step(0, 0, True, False)

    def attn_pair(j, carry):
        attn_step(2 * j + 1, 1, True, True)
        attn_step(2 * j + 2, 0, True, True)
        return carry

    lax.fori_loop(0, qb // 2, attn_pair, 0)

    @pl.when(qb % 2 == 1)
    def _():
        attn_step(qb, 1, True, True)
        attn_step(qb + 1, 0, False, True)

    @pl.when(qb % 2 == 0)
    def _():
        attn_step(qb + 1, 1, False, True)

    for h in range(DSA_HEADS):
        acc = acc_scr[h].reshape(DSA_LAT_EXT, tq)
        o_t = (acc[0:DSA_LATENT] / acc[DSA_LATENT:DSA_LATENT + 1]).astype(BF16)
        u_scr[h * DSA_VDIM:(h + 1) * DSA_VDIM, :] = _nn(wuv_ref[h], o_t).astype(BF16)
    y_t = _nn(wo_ref[...], u_scr[...])
    o_ref[...] = x_ref[...] + y_t.T


def _dsa_attn(x, q, qi, wt, ki, kv, kvt, bias_t, wuv_t, wo_t, *, bsz, s):
    t, d = x.shape
    tq = DSA_TQ
    nq = s // tq
    topk = min(TOPK_MAX, s // 4)
    full = lambda a: _resident(a.shape)
    return pl.pallas_call(
        functools.partial(_dsa_attn_body, topk=topk),
        grid=(bsz, nq),
        in_specs=[
            pl.BlockSpec((tq, q.shape[1]), lambda b, i: (b * nq + i, 0)),
            pl.BlockSpec((tq, qi.shape[1]), lambda b, i: (b * nq + i, 0)),
            pl.BlockSpec((IDX_HEADS, tq), lambda b, i: (0, b * nq + i)),
            pl.BlockSpec((s, IDX_DIM), lambda b, i: (b, 0)),
            pl.BlockSpec((s, DSA_LATENT), lambda b, i: (b, 0)),
            pl.BlockSpec((DSA_LAT_EXT, s), lambda b, i: (0, b)),
            full(bias_t),
            pl.BlockSpec((tq, d), lambda b, i: (b * nq + i, 0)),
            full(wuv_t), full(wo_t),
        ],
        out_specs=pl.BlockSpec((tq, d), lambda b, i: (b * nq + i, 0)),
        out_shape=jax.ShapeDtypeStruct((t, d), F32),
        scratch_shapes=[
            pltpu.VMEM((s + tq, tq), I32),
            pltpu.VMEM((s + tq, tq), I16),
            pltpu.VMEM((s + tq, tq), I16),
            pltpu.VMEM((s + tq, tq), I16),
            pltpu.VMEM((2, DSA_HEADS, tq // SUBLANES, SUBLANES, tq), F32),
            pltpu.VMEM((DSA_HEADS, DSA_LAT_EXT // SUBLANES, SUBLANES, tq), F32),
            pltpu.VMEM((DSA_HEADS, SUBLANES, tq), F32),
            pltpu.VMEM((DSA_HEADS, SUBLANES, tq), F32),
            pltpu.VMEM((2, DSA_HEADS, SUBLANES, tq), F32),
            pltpu.VMEM((2, DSA_HEADS, SUBLANES, tq), F32),
            pltpu.VMEM((DSA_HEADS * DSA_VDIM, tq), BF16),
        ],
        compiler_params=_params("parallel", "arbitrary"),
        name="dsa_attn",
    )(q, qi, wt, ki, kv, kvt, bias_t, x, wuv_t, wo_t)


def _t5_bucket(rel):
    max_exact = REL_BUCKETS // 2
    relf = jnp.maximum(rel, 1).astype(F32)
    large = max_exact + (jnp.log(relf / max_exact) / math.log(REL_MAX_DIST / max_exact)
                         * (REL_BUCKETS - max_exact)).astype(I32)
    large = jnp.minimum(large, REL_BUCKETS - 1)
    return jnp.where(rel < max_exact, rel, large)


def _bias_tiles(rel_bias):
    tq = DSA_TQ
    kk = jnp.arange(tq)[:, None]
    qq = jnp.arange(tq)[None, :]
    rel = jnp.stack([qq - kk, qq - kk + tq])
    table = (rel_bias - rel_bias[REL_BUCKETS - 1][None, :]) * LOG2E
    onehot = (_t5_bucket(jnp.maximum(rel, 0))[..., None] == jnp.arange(REL_BUCKETS)).astype(F32)
    return jnp.einsum("kabn,nh->hkab", onehot, table, precision=lax.Precision.HIGHEST)


def _hy_in_weight(w_in):
    sizes = (GLA_QK, GLA_QK, GLA_V, GLA_V, GLA_GATE_RANK, SSD_INNER, SSD_INNER + 2 * SSD_BC, SSD_HEADS)
    offs = np.cumsum((0,) + sizes)
    q, k, v, r, g_lr, z, xbc, dt = [w_in[:, offs[i]:offs[i + 1]] for i in range(len(sizes))]
    pad = jnp.zeros((w_in.shape[0], HY_SMALL - GLA_GATE_RANK - SSD_HEADS), w_in.dtype)
    return jnp.concatenate([q, k, v, r, z, xbc, g_lr, dt, pad], axis=1).astype(BF16)


def kernel(x, ffn_norm_g, ffn_w_gate, ffn_w_up, ffn_w_down, mix_norm_g, hy_w_in, gla_w_gate2, gla_b_gate, gla_norm_g, ssd_conv_w, ssd_conv_b, ssd_dt_bias, ssd_a_log, ssd_d, ssd_norm_g, hy_w_out, dsa_w_dq, dsa_q_norm_g, dsa_w_uq, dsa_w_dkv, dsa_kv_norm_g, dsa_w_uv, dsa_w_o, idx_w_q, idx_w_k, idx_ln_g, idx_ln_b, idx_w_w, rel_bias, final_norm_g):
    bsz, s, d = x.shape
    t = bsz * s
    xt = x.reshape(t, d)
    row = lambda a: a.reshape(1, -1).astype(F32)

    wg_all, wu_all, wd_all = (w.astype(BF16) for w in (ffn_w_gate, ffn_w_up, ffn_w_down))

    def ffn(xt, layer, half, final, mix=None):
        return _ffn(xt, row(ffn_norm_g[layer, half]), wg_all, wu_all, wd_all, row(final_norm_g),
                    layer=layer, half=half, final_norm=final, mix=mix)

    xt = ffn(xt, 0, 0, False)
    p, sm = _in_proj(xt, row(mix_norm_g[0]), _hy_in_weight(hy_w_in[0]))
    p = p.reshape(bsz, s, HY_MAIN)
    sm = sm.reshape(bsz, s, HY_SMALL)
    wg2 = jnp.zeros((HY_SMALL, GLA_QK), F32).at[:GLA_GATE_RANK].set(gla_w_gate2[0]).astype(BF16)
    o_gla = _gla(p, sm, wg2, row(gla_b_gate[0]), row(gla_norm_g[0]))
    cw, cb = ssd_conv_w[0].astype(F32), ssd_conv_b[0].astype(F32)
    lane_pad = lambda v: jnp.zeros((1, HY_SMALL), F32).at[0, HY_DT_LANE:HY_DT_LANE + SSD_HEADS].set(v)
    y_ssd = _ssd(p, sm, cw[:, :SSD_INNER], row(cb[:SSD_INNER]), cw[:, SSD_INNER:], row(cb[SSD_INNER:]),
                 lane_pad(ssd_dt_bias[0].astype(F32)), lane_pad(-jnp.exp(ssd_a_log[0].astype(F32))),
                 row(jnp.repeat(ssd_d[0], SSD_HEADDIM)), row(ssd_norm_g[0]))
    w_out = hy_w_out[0].astype(BF16)
    xt = ffn(xt, 0, 1, False,
             mix=(o_gla.reshape(t, GLA_V), y_ssd.reshape(t, SSD_INNER), w_out[:GLA_V], w_out[GLA_V:]))

    xt = ffn(xt, 1, 0, False)
    wh = jnp.concatenate(
        [dsa_w_dq[0], dsa_w_dkv[0], idx_w_k[0], idx_w_w[0],
         jnp.zeros((d, DSA_H_WIDTH - DSA_Q_RANK - DSA_LATENT - IDX_DIM - IDX_HEADS), F32)], axis=1).astype(BF16)
    wq = jnp.concatenate([dsa_w_uq[0], idx_w_q[0]], axis=1).astype(BF16)
    q, qi, kv, kvt, ki, wt = _dsa_proj(xt, row(mix_norm_g[1]), wh, wq, row(dsa_q_norm_g[0]),
                                       row(dsa_kv_norm_g[0]), row(idx_ln_g[0]), row(idx_ln_b[0]))
    wuv_t = jnp.transpose(dsa_w_uv[0], (0, 2, 1)).astype(BF16)
    wo_t = dsa_w_o[0].T.astype(BF16)
    xt = _dsa_attn(xt, q, qi, wt, ki, kv, kvt, _bias_tiles(rel_bias.astype(F32)), wuv_t, wo_t, bsz=bsz, s=s)
    xt = ffn(xt, 1, 1, True)
    return xt.reshape(bsz, s, d)
```
